```python
import math, functools
import jax, jax.numpy as jnp
from jax import lax
import numpy as np

D_MODEL = 2048
BATCH = 8
SEQ = 2048
DEPTH = 2
DEC_BATCH = 32
DEC_SEQ = 8
PAST_LEN = 8192
PAGE_SIZE = 128

D_MIX = D_MODEL
HEAD_DIM = 128
D_POOL = D_MIX // 4
POOL_WINDOWS = (2, 4, 8, 16)
POOL_GROUP = D_POOL // len(POOL_WINDOWS)
POOL_HIST = max(POOL_WINDOWS) - 1
D_HGRN = (D_MIX - D_POOL) // 2
H_HGRN = D_HGRN // HEAD_DIM
D_ATT = D_MIX - D_POOL - D_HGRN
H_ATT = D_ATT // HEAD_DIM
HGRN_CHUNK = 64
MOBA_BLOCK = 256
MOBA_TOPK = 3
MOBA_Q_CHUNK = 16
NORM_EPS = 1e-6
IN_COLS = 2 * D_POOL + 4 * D_HGRN + 4 * D_ATT

kernel_name = 'hymba_pool_hgrn2_moba_step'


def _rms(x, g):
    xf = x.astype(jnp.float32)
    y = xf * lax.rsqrt(jnp.mean(xf * xf, axis=-1, keepdims=True) + NORM_EPS)
    return (y * g.astype(jnp.float32)).astype(x.dtype)


def _alibi_slopes(n):
    def geo(m):
        start = 2.0 ** (-8.0 / m)
        return [start ** (i + 1) for i in range(m)]
    if (n & (n - 1)) == 0:
        s = geo(n)
    else:
        c = 2 ** int(math.floor(math.log2(n)))
        s = geo(c) + geo(2 * c)[0::2][: n - c]
    return jnp.asarray(s, dtype=jnp.float32)


def _to_heads(a, n):
    b, l, _ = a.shape
    return a.reshape(b, l, n, HEAD_DIM).transpose(0, 2, 1, 3)


def _from_heads(a):
    b, h, l, d = a.shape
    return a.transpose(0, 2, 1, 3).reshape(b, l, h * d)


def _split_cols(z):
    sizes = (D_POOL, D_POOL, D_HGRN, D_HGRN, D_HGRN, D_HGRN, D_ATT, D_ATT, D_ATT, D_ATT)
    offs = np.cumsum(sizes)[:-1].tolist()
    return jnp.split(z, offs, axis=-1)


def _pool_mixer(u, hist, pos0, w_grp, scale):
    b, l, _ = u.shape
    ext = jnp.concatenate([hist.astype(jnp.float32), u.astype(jnp.float32)], axis=1)
    csum = jnp.concatenate([jnp.zeros_like(ext[:, :1]), jnp.cumsum(ext, axis=1)], axis=1)
    pos = pos0 + jnp.arange(l)
    end = csum[:, POOL_HIST + 1:]
    means = []
    for gi, w in enumerate(POOL_WINDOWS):
        cs = slice(gi * POOL_GROUP, (gi + 1) * POOL_GROUP)
        start = csum[:, POOL_HIST + 1 - w: POOL_HIST + 1 - w + l, cs]
        cnt = jnp.minimum(w, pos + 1).astype(jnp.float32)[None, :, None]
        means.append((end[..., cs] - start) / cnt)
    diff = (jnp.concatenate(means, axis=-1) - u.astype(jnp.float32)).reshape(b, l, len(POOL_WINDOWS), POOL_GROUP)
    out = jnp.einsum('blgc,gcd->blgd', diff, w_grp.astype(jnp.float32)).reshape(b, l, D_POOL)
    out = out * scale.astype(jnp.float32)
    return out.astype(u.dtype), ext[:, -POOL_HIST:].astype(hist.dtype)


def _gla_chunked(q, k, v, logf, s0):
    b, h, l, dk = q.shape
    dv = v.shape[-1]
    c = HGRN_CHUNK if l % HGRN_CHUNK == 0 else l
    n = l // c

    def chunks(a):
        return a.reshape(b, h, n, c, a.shape[-1]).transpose(2, 0, 1, 3, 4)

    causal = jnp.tril(jnp.ones((c, c), dtype=bool))

    def step(s, inp):
        qc, kc, vc, gc = inp
        cum = jnp.cumsum(gc, axis=2)
        inter = jnp.einsum('bhtk,bhkv->bhtv', qc * jnp.exp(cum), s)
        dec = jnp.where(causal[:, :, None], cum[:, :, :, None, :] - cum[:, :, None, :, :], -jnp.inf)
        att = jnp.einsum('bhtk,bhsk,bhtsk->bhts', qc, kc, jnp.exp(dec))
        o = inter + jnp.einsum('bhts,bhsv->bhtv', att, vc)
        last = cum[:, :, -1, :]
        s = jnp.exp(last)[..., None] * s + jnp.einsum('bhsk,bhsv->bhkv', kc * jnp.exp(last[:, :, None, :] - cum), vc)
        return s, o

    s, o = lax.scan(step, s0, (chunks(q), chunks(k), chunks(v), chunks(logf)))
    return o.transpose(1, 2, 0, 3, 4).reshape(b, h, l, dv), s


def _hgrn2(q_raw, f_raw, i_raw, s0, lb, norm_g):
    f32 = jnp.float32
    z = _to_heads(f_raw, H_HGRN).astype(f32)
    lbh = lb.astype(f32).reshape(H_HGRN, 1, HEAD_DIM)
    logf = jnp.logaddexp(jnp.log(lbh), jnp.log1p(-lbh) + jax.nn.log_sigmoid(z))
    k = (1.0 - lbh) * jax.nn.sigmoid(-z)
    q = jax.nn.silu(_to_heads(q_raw, H_HGRN).astype(f32))
    v = _to_heads(i_raw, H_HGRN).astype(f32)
    o, s = _gla_chunked(q, k, v, logf, s0.astype(f32))
    o = _rms(o, norm_g.reshape(H_HGRN, 1, HEAD_DIM))
    return _from_heads(o).astype(q_raw.dtype), s.astype(s0.dtype)


def _moba_attend(q, qpos, own_k, own_v, own_pos, slopes, sel):
    scale = HEAD_DIM ** -0.5
    sl = slopes[None, :, None, None]
    qp = qpos.astype(jnp.float32)
    lo = jnp.einsum('bhqd,bhod->bhqo', q, own_k).astype(jnp.float32) * scale
    lo = lo - sl * (qp[:, None] - own_pos.astype(jnp.float32)[None, :])
    lo = jnp.where(own_pos[None, :] <= qpos[:, None], lo, -jnp.inf)
    if sel is None:
        p = jax.nn.softmax(lo, axis=-1)
        return jnp.einsum('bhqo,bhod->bhqd', p.astype(own_v.dtype), own_v)
    k_sel, v_sel, pos_sel, valid = sel
    ls = jnp.einsum('bhqd,bhqsd->bhqs', q, k_sel).astype(jnp.float32) * scale
    ls = ls - sl * (qp[None, None, :, None] - pos_sel.astype(jnp.float32))
    ls = jnp.where(valid, ls, -jnp.inf)
    n_s = ls.shape[-1]
    p = jax.nn.softmax(jnp.concatenate([ls, lo], axis=-1), axis=-1).astype(own_v.dtype)
    return (jnp.einsum('bhqs,bhqsd->bhqd', p[..., :n_s], v_sel)
            + jnp.einsum('bhqo,bhod->bhqd', p[..., n_s:], own_v))


def _moba_prompt(q, k, v, slopes):
    b, h, t, d = q.shape
    nb = -(-t // MOBA_BLOCK)
    pad = ((0, 0), (0, 0), (0, nb * MOBA_BLOCK - t), (0, 0))
    kp = jnp.pad(k, pad)
    vp = jnp.pad(v, pad)
    kb = kp.reshape(b, h, nb, MOBA_BLOCK, d)
    vb = vp.reshape(b, h, nb, MOBA_BLOCK, d)
    kmean = jnp.mean(kb.astype(jnp.float32), axis=3)
    n_sel = min(MOBA_TOPK, nb - 1)
    nq = t // MOBA_Q_CHUNK
    qc = q.reshape(b, h, nq, MOBA_Q_CHUNK, d).transpose(2, 0, 1, 3, 4)
    bi = jnp.arange(b)[:, None, None, None]
    hi = jnp.arange(h)[None, :, None, None]

    def one(args):
        qblk, ci = args
        t0 = ci * MOBA_Q_CHUNK
        qpos = t0 + jnp.arange(MOBA_Q_CHUNK)
        own = t0 // MOBA_BLOCK
        own_k = lax.dynamic_slice_in_dim(kp, own * MOBA_BLOCK, MOBA_BLOCK, axis=2)
        own_v = lax.dynamic_slice_in_dim(vp, own * MOBA_BLOCK, MOBA_BLOCK, axis=2)
        own_pos = own * MOBA_BLOCK + jnp.arange(MOBA_BLOCK)
        sel = None
        if n_sel > 0:
            sc = jnp.einsum('bhqd,bhnd->bhqn', qblk.astype(jnp.float32), kmean)
            sc = jnp.where(jnp.arange(nb) < own, sc, -jnp.inf)
            _, idx = lax.top_k(sc, n_sel)
            k_sel = kb[bi, hi, idx].reshape(b, h, MOBA_Q_CHUNK, n_sel * MOBA_BLOCK, d)
            v_sel = vb[bi, hi, idx].reshape(b, h, MOBA_Q_CHUNK, n_sel * MOBA_BLOCK, d)
            pos_sel = (idx[..., None] * MOBA_BLOCK + jnp.arange(MOBA_BLOCK)).reshape(b, h, MOBA_Q_CHUNK, n_sel * MOBA_BLOCK)
            valid = jnp.repeat(jnp.arange(n_sel) < own, MOBA_BLOCK)
            sel = (k_sel, v_sel, pos_sel, valid)
        return _moba_attend(qblk, qpos, own_k, own_v, own_pos, slopes, sel)

    o = lax.map(one, (qc, jnp.arange(nq)))
    return o.transpose(1, 2, 0, 3, 4).reshape(b, h, t, d)


def _moba_sample(q, k_new, v_new, k_pool, v_pool, page_table, slopes):
    b, h, l, d = q.shape
    n_pages = page_table.shape[1]
    past = n_pages * PAGE_SIZE
    ppb = MOBA_BLOCK // PAGE_SIZE
    c = past // MOBA_BLOCK
    r_pages = n_pages - c * ppb
    qpos = past + jnp.arange(l)
    pref = page_table[:, c * ppb:]
    pk = k_pool[pref].transpose(0, 2, 1, 3, 4).reshape(b, h, r_pages * PAGE_SIZE, d)
    pv = v_pool[pref].transpose(0, 2, 1, 3, 4).reshape(b, h, r_pages * PAGE_SIZE, d)
    own_k = jnp.concatenate([pk.astype(k_new.dtype), k_new], axis=2)
    own_v = jnp.concatenate([pv.astype(v_new.dtype), v_new], axis=2)
    own_pos = c * MOBA_BLOCK + jnp.arange(r_pages * PAGE_SIZE + l)
    n_sel = min(MOBA_TOPK, c)
    sel = None
    if n_sel > 0:
        past_k = k_pool[page_table[:, : c * ppb]].astype(jnp.float32)
        kmean = jnp.mean(past_k.reshape(b, c, ppb, h, PAGE_SIZE, d), axis=(2, 4)).transpose(0, 2, 1, 3)
        sc = jnp.einsum('bhqd,bhnd->bhqn', q.astype(jnp.float32), kmean)
        _, idx = lax.top_k(sc, n_sel)
        lp = idx[..., None] * ppb + jnp.arange(ppb)
        phys = page_table[jnp.arange(b)[:, None, None, None, None], lp]
        hi = jnp.arange(h)[None, :, None, None, None]
        k_sel = k_pool[phys, hi].reshape(b, h, l, n_sel * MOBA_BLOCK, d).astype(k_new.dtype)
        v_sel = v_pool[phys, hi].reshape(b, h, l, n_sel * MOBA_BLOCK, d).astype(v_new.dtype)
        pos_sel = (lp[..., None] * PAGE_SIZE + jnp.arange(PAGE_SIZE)).reshape(b, h, l, n_sel * MOBA_BLOCK)
        sel = (k_sel, v_sel, pos_sel, jnp.ones((n_sel * MOBA_BLOCK,), dtype=bool))
    return _moba_attend(q, qpos, own_k, own_v, own_pos, slopes, sel)


def _layer(x, pos0, norm_g, w_in, pool_w, pool_scale, lb, hgrn_g, qn_g, kn_g, w_out,
           pool_hist, hgrn_s0, attn_fn):
    h = _rms(x, norm_g)
    z = jnp.einsum('bld,de->ble', h, w_in)
    u_a, g_a, q_b, f_b, i_b, g_b, q_c, k_c, v_c, g_c = _split_cols(z)
    o_a, hist_new = _pool_mixer(u_a, pool_hist, pos0, pool_w, pool_scale)
    o_b, s_new = _hgrn2(q_b, f_b, i_b, hgrn_s0, lb, hgrn_g)
    q = _rms(_to_heads(q_c, H_ATT), qn_g)
    k = _rms(_to_heads(k_c, H_ATT), kn_g)
    v = _to_heads(v_c, H_ATT)
    o_c = _from_heads(attn_fn(q, k, v))
    mix = jnp.concatenate([jax.nn.silu(g_a) * o_a, jax.nn.silu(g_b) * o_b, jax.nn.silu(g_c) * o_c], axis=-1)
    y = x + jnp.einsum('ble,ed->bld', mix, w_out)
    return y, hist_new, s_new, k, v


def setup_inputs(seed: int = 0) -> dict:
    key = jax.random.key(seed)
    ks = jax.random.split(key, 17)
    f32 = jnp.float32
    n_pages = PAST_LEN // PAGE_SIZE
    n_used = DEC_BATCH * n_pages
    n_phys = n_used + max(1, n_used // 4)
    nrm = lambda k, s: jax.random.normal(k, s, dtype=f32)
    page_table = jax.random.permutation(ks[6], n_phys)[:n_used].reshape(DEC_BATCH, n_pages).astype(jnp.int32)
    return {
        'x_prompt': nrm(ks[0], (BATCH, SEQ, D_MODEL)),
        'x_sample': nrm(ks[1], (DEC_BATCH, DEC_SEQ, D_MODEL)),
        'cache_k': nrm(ks[2], (DEPTH, n_phys, H_ATT, PAGE_SIZE, HEAD_DIM)),
        'cache_v': nrm(ks[3], (DEPTH, n_phys, H_ATT, PAGE_SIZE, HEAD_DIM)),
        'state_hgrn': 0.5 * nrm(ks[4], (DEPTH, DEC_BATCH, H_HGRN, HEAD_DIM, HEAD_DIM)),
        'state_pool': nrm(ks[5], (DEPTH, DEC_BATCH, POOL_HIST, D_POOL)),
        'page_table': page_table,
        'norm_gain': 1.0 + 0.1 * nrm(ks[7], (DEPTH, D_MODEL)),
        'w_in': nrm(ks[8], (DEPTH, D_MODEL, IN_COLS)) * D_MODEL ** -0.5,
        'pool_w': nrm(ks[9], (DEPTH, len(POOL_WINDOWS), POOL_GROUP, POOL_GROUP)) * POOL_GROUP ** -0.5,
        'pool_scale': 1.0 + 0.1 * nrm(ks[10], (DEPTH, D_POOL)),
        'hgrn_lb': 0.5 * nrm(ks[11], (DEPTH, D_HGRN)),
        'hgrn_norm_gain': 1.0 + 0.1 * nrm(ks[12], (DEPTH, D_HGRN)),
        'q_norm_gain': 1.0 + 0.1 * nrm(ks[13], (DEPTH, HEAD_DIM)),
        'k_norm_gain': 1.0 + 0.1 * nrm(ks[14], (DEPTH, HEAD_DIM)),
        'w_out': nrm(ks[15], (DEPTH, D_MIX, D_MODEL)) * D_MIX ** -0.5,
    }


def reference(x_prompt, x_sample, cache_k, cache_v, state_hgrn, state_pool, page_table,
              norm_gain, w_in, pool_w, pool_scale, hgrn_lb, hgrn_norm_gain,
              q_norm_gain, k_norm_gain, w_out):
    slopes = _alibi_slopes(H_ATT)
    lb_cum = jnp.cumsum(jax.nn.softmax(hgrn_lb.astype(jnp.float32), axis=0), axis=0)
    lb_all = lb_cum - lb_cum[:1]
    b, t = x_prompt.shape[0], x_prompt.shape[1]
    past_len = page_table.shape[1] * PAGE_SIZE
    y_p, y_s = x_prompt, x_sample
    kp_l, vp_l, hp_l, pp_l, ks_l, vs_l, hs_l, ps_l = [], [], [], [], [], [], [], []
    for l in range(DEPTH):
        lw = (norm_gain[l], w_in[l], pool_w[l], pool_scale[l], lb_all[l], hgrn_norm_gain[l],
              q_norm_gain[l], k_norm_gain[l], w_out[l])
        y_p, hist_p, s_p, k_p, v_p = _layer(
            y_p, 0, *lw,
            jnp.zeros((b, POOL_HIST, D_POOL), y_p.dtype),
            jnp.zeros((b, H_HGRN, HEAD_DIM, HEAD_DIM), state_hgrn.dtype),
            functools.partial(_moba_prompt, slopes=slopes))
        y_s, hist_s, s_s, k_s, v_s = _layer(
            y_s, past_len, *lw, state_pool[l], state_hgrn[l],
            functools.partial(_moba_sample, k_pool=cache_k[l], v_pool=cache_v[l],
                              page_table=page_table, slopes=slopes))
        kp_l.append(k_p.reshape(b, H_ATT, t // PAGE_SIZE, PAGE_SIZE, HEAD_DIM).transpose(0, 2, 1, 3, 4))
        vp_l.append(v_p.reshape(b, H_ATT, t // PAGE_SIZE, PAGE_SIZE, HEAD_DIM).transpose(0, 2, 1, 3, 4))
        hp_l.append(s_p)
        pp_l.append(hist_p)
        ks_l.append(k_s)
        vs_l.append(v_s)
        hs_l.append(s_s)
        ps_l.append(hist_s)
    new_k_prompt = jnp.stack(kp_l)
    new_v_prompt = jnp.stack(vp_l)
    new_hgrn_prompt = jnp.stack(hp_l)
    new_pool_prompt = jnp.stack(pp_l)
    new_k_sample = jnp.stack(ks_l)
    new_v_sample = jnp.stack(vs_l)
    new_hgrn_sample = jnp.stack(hs_l)
    new_pool_sample = jnp.stack(ps_l)
    return (y_p, y_s, new_k_prompt, new_v_prompt, new_hgrn_prompt, new_pool_prompt,
            new_k_sample, new_v_sample, new_hgrn_sample, new_pool_sample)
```

```python
import functools
import math

import jax
import jax.numpy as jnp
import numpy as np
from jax import lax
from jax.experimental import pallas as pl
from jax.experimental.pallas import tpu as pltpu

F32 = jnp.float32
BF16 = jnp.bfloat16

HEAD_DIM = 128
POOL_WINDOWS = (2, 4, 8, 16)
POOL_HIST = max(POOL_WINDOWS) - 1
HIST_ROWS = POOL_HIST + 1
HGRN_CHUNK = 64
HGRN_SUB = 16
MOBA_BLOCK = 256
MOBA_TOPK = 3
PAGE_SIZE = 128
PAGES_PER_BLOCK = MOBA_BLOCK // PAGE_SIZE
NORM_EPS = 1e-6
MASKED = -1e30
VMEM_LIMIT = 48 * 1024 * 1024


def _pick(n, candidates):
    for c in candidates:
        if n % c == 0:
            return c
    raise ValueError(f"no tile for {n}")


def _params(*sem):
    return pltpu.CompilerParams(dimension_semantics=sem, vmem_limit_bytes=VMEM_LIMIT)


def _silu(x):
    return x / (1.0 + jnp.exp(-x))


def _rms_rows(x, g):
    ms = jnp.mean(x * x, axis=-1, keepdims=True)
    return x * lax.rsqrt(ms + NORM_EPS) * g


def _inproj_kernel(x_ref, g_ref, w_ref, z_ref, h_ref):
    @pl.when(pl.program_id(1) == 0)
    def _():
        h_ref[...] = _rms_rows(x_ref[...], g_ref[...]).astype(BF16)
    z_ref[...] = jnp.dot(h_ref[...], w_ref[...], preferred_element_type=F32)


def _inproj(x2d, gain, w):
    m, d = x2d.shape
    n = w.shape[1]
    tm = _pick(m, (1024, 512, 256, 128, 64, 32, 16, 8))
    tn = _pick(n, (512, 256, 128))
    return pl.pallas_call(
        _inproj_kernel,
        grid=(m // tm, n // tn),
        in_specs=[pl.BlockSpec((tm, d), lambda i, j: (i, 0)),
                  pl.BlockSpec((1, d), lambda i, j: (0, 0)),
                  pl.BlockSpec((d, tn), lambda i, j: (0, j))],
        out_specs=pl.BlockSpec((tm, tn), lambda i, j: (i, j)),
        out_shape=jax.ShapeDtypeStruct((m, n), F32),
        scratch_shapes=[pltpu.VMEM((tm, d), BF16)],
        compiler_params=_params("arbitrary", "arbitrary"),
        name="inproj",
    )(x2d, gain, w)


def _outproj_kernel(x_ref, a_ref, b_ref, c_ref, wa_ref, wb_ref, wc_ref, y_ref):
    acc = jnp.dot(a_ref[...], wa_ref[...], preferred_element_type=F32)
    acc = acc + jnp.dot(b_ref[...], wb_ref[...], preferred_element_type=F32)
    acc = acc + jnp.dot(c_ref[...], wc_ref[...], preferred_element_type=F32)
    y_ref[...] = x_ref[...] + acc


def _outproj(x2d, oa, ob, oc, wa, wb, wc):
    m, d = x2d.shape
    tm = _pick(m, (1024, 512, 256, 128, 64, 32, 16, 8))
    tn = _pick(d, (512, 256, 128))
    row = lambda a: pl.BlockSpec((tm, a.shape[1]), lambda i, j: (i, 0))
    col = lambda w: pl.BlockSpec((w.shape[0], tn), lambda i, j: (0, j))
    return pl.pallas_call(
        _outproj_kernel,
        grid=(m // tm, d // tn),
        in_specs=[pl.BlockSpec((tm, tn), lambda i, j: (i, j)),
                  row(oa), row(ob), row(oc), col(wa), col(wb), col(wc)],
        out_specs=pl.BlockSpec((tm, tn), lambda i, j: (i, j)),
        out_shape=jax.ShapeDtypeStruct((m, d), F32),
        compiler_params=_params("arbitrary", "arbitrary"),
        name="outproj",
    )(x2d, oa, ob, oc, wa, wb, wc)


def _pool_kernel(u_ref, g_ref, hist_ref, w_ref, scale_ref, o_ref, hist_out_ref, ext_ref,
                 *, tl, pos0, group):
    t = pl.program_id(1)

    @pl.when(t == 0)
    def _():
        ext_ref[0:HIST_ROWS, :] = hist_ref[...]

    @pl.when(t > 0)
    def _():
        ext_ref[0:HIST_ROWS, :] = ext_ref[tl:tl + HIST_ROWS, :]

    ext_ref[HIST_ROWS:HIST_ROWS + tl, :] = u_ref[...]
    pos = pos0 + t * tl + lax.broadcasted_iota(jnp.int32, (tl, group), 0)
    for gi, w in enumerate(POOL_WINDOWS):
        cs = slice(gi * group, (gi + 1) * group)
        acc = ext_ref[HIST_ROWS:HIST_ROWS + tl, cs]
        for j in range(1, w):
            acc = acc + ext_ref[HIST_ROWS - j:HIST_ROWS - j + tl, cs]
        cnt = jnp.minimum(w, pos + 1).astype(F32)
        diff = acc / cnt - u_ref[:, cs]
        out = jnp.dot(diff.astype(BF16), w_ref[gi], preferred_element_type=F32) * scale_ref[:, cs]
        o_ref[:, cs] = (_silu(g_ref[:, cs]) * out).astype(BF16)
    hist_out_ref[...] = ext_ref[tl:tl + HIST_ROWS, :]


def _pool(z3, hist16, w_bf16, scale, pos0, d_pool):
    b, l, _ = z3.shape
    group = d_pool // len(POOL_WINDOWS)
    tl = _pick(l, (512, 256, 128, 64, 32, 16, 8))
    kern = functools.partial(_pool_kernel, tl=tl, pos0=pos0, group=group)
    return pl.pallas_call(
        kern,
        grid=(b, l // tl),
        in_specs=[pl.BlockSpec((None, tl, d_pool), lambda i, t: (i, t, 0)),
                  pl.BlockSpec((None, tl, d_pool), lambda i, t: (i, t, 1)),
                  pl.BlockSpec((None, HIST_ROWS, d_pool), lambda i, t: (i, 0, 0)),
                  pl.BlockSpec((len(POOL_WINDOWS), group, group), lambda i, t: (0, 0, 0)),
                  pl.BlockSpec((1, d_pool), lambda i, t: (0, 0))],
        out_specs=[pl.BlockSpec((None, tl, d_pool), lambda i, t: (i, t, 0)),
                   pl.BlockSpec((None, HIST_ROWS, d_pool), lambda i, t: (i, 0, 0))],
        out_shape=[jax.ShapeDtypeStruct((b, l, d_pool), BF16),
                   jax.ShapeDtypeStruct((b, HIST_ROWS, d_pool), F32)],
        scratch_shapes=[pltpu.VMEM((HIST_ROWS + tl, d_pool), F32)],
        compiler_params=_params("arbitrary", "arbitrary"),
        name="pool",
    )(z3, z3, hist16, w_bf16, scale)


def _cumsum_rows(x):
    n = x.shape[0]
    row = lax.broadcasted_iota(jnp.int32, x.shape, 0)
    s = 1
    while s < n:
        x = x + jnp.where(row >= s, pltpu.roll(x, s, axis=0), 0.0)
        s *= 2
    return x


def _hgrn_chunk(q_raw, z, v, lb_terms, st_ref, consts, chunk, sub, live=None):
    log_lb, log1m_lb, one_m_lb = lb_terms
    ones_kk, diag_mask, col_blk = consts
    lsig = jnp.minimum(z, 0.0) - jnp.log1p(jnp.exp(-jnp.abs(z)))
    b_ = log1m_lb + lsig
    logf = jnp.maximum(log_lb, b_) + jnp.log1p(jnp.exp(-jnp.abs(log_lb - b_)))
    k = one_m_lb * (1.0 / (1.0 + jnp.exp(z)))
    if live is not None:
        logf = jnp.where(live, logf, 0.0)
        k = jnp.where(live, k, 0.0)
    q = _silu(q_raw)
    cum = _cumsum_rows(logf)
    last = cum[chunk - 1:chunk, :]
    st = st_ref[...]
    o = lax.dot_general((q * jnp.exp(cum)).astype(BF16), st.astype(BF16),
                        (((1,), (1,)), ((), ())), preferred_element_type=F32)
    nsub = chunk // sub
    vb16 = v.astype(BF16)
    if nsub > 1:
        atts = [jnp.zeros((sub, chunk), F32)]
        for i in range(1, nsub):
            c_i = cum[i * sub - 1:i * sub, :]
            a = q[i * sub:(i + 1) * sub] * jnp.exp(cum[i * sub:(i + 1) * sub] - c_i)
            bmat = k * jnp.exp(jnp.minimum(c_i - cum, 0.0))
            att = lax.dot_general(a.astype(BF16), bmat.astype(BF16),
                                  (((1,), (1,)), ((), ())), preferred_element_type=F32)
            atts.append(jnp.where(col_blk < i, att, 0.0))
        att_all = jnp.concatenate(atts, axis=0)
        o = o + jnp.dot(att_all.astype(BF16), vb16, preferred_element_type=F32)
    diag = []
    for i in range(nsub):
        sl = slice(i * sub, (i + 1) * sub)
        cb, kb, qb, vb = cum[sl], k[sl], q[sl], v[sl]
        ws = []
        for t in range(sub):
            e = jnp.exp(jnp.minimum(cb[t:t + 1, :] - cb, 0.0))
            ws.append((qb[t:t + 1, :] * kb) * e)
        w = jnp.concatenate(ws, axis=0)
        att_rep = jnp.dot(w.astype(BF16), ones_kk, preferred_element_type=F32)
        p = jnp.where(diag_mask, att_rep * jnp.concatenate([vb] * sub, axis=0), 0.0)
        diag.append(jnp.sum(p.reshape(sub, sub, HEAD_DIM), axis=1))
    o = o + (jnp.concatenate(diag, axis=0) if nsub > 1 else diag[0])
    kd = (k * jnp.exp(last - cum)).astype(BF16)
    upd = lax.dot_general(vb16, kd, (((0,), (0,)), ((), ())), preferred_element_type=F32)
    st_ref[...] = st * jnp.exp(last) + upd
    return o


def _hgrn_consts(chunk, sub):
    ones_kk = jnp.ones((HEAD_DIM, HEAD_DIM), BF16)
    r = lax.broadcasted_iota(jnp.int32, (sub * sub, HEAD_DIM), 0)
    diag_mask = (r % sub) <= (r // sub)
    col_blk = lax.broadcasted_iota(jnp.int32, (sub, chunk), 1) // sub
    return ones_kk, diag_mask, col_blk


def _lb_terms(lb):
    return jnp.log(lb), jnp.log1p(-lb), 1.0 - lb


def _hgrn_prompt_kernel(q_ref, f_ref, i_ref, g_ref, lb_ref, ng_ref, s0_ref, o_ref, s_out_ref,
                        st_ref, *, tl, chunk, sub):
    t = pl.program_id(2)

    @pl.when(t == 0)
    def _():
        st_ref[...] = s0_ref[...].T

    lbt = _lb_terms(lb_ref[...])
    consts = _hgrn_consts(chunk, sub)
    ng = ng_ref[...]

    def body(c, carry):
        rows = pl.ds(pl.multiple_of(c * chunk, chunk), chunk)
        o = _hgrn_chunk(q_ref[rows, :], f_ref[rows, :], i_ref[rows, :], lbt, st_ref, consts, chunk, sub)
        o_ref[rows, :] = (_silu(g_ref[rows, :]) * _rms_rows(o, ng)).astype(BF16)
        return carry

    lax.fori_loop(0, tl // chunk, body, 0)

    @pl.when(t == pl.num_programs(2) - 1)
    def _():
        s_out_ref[...] = st_ref[...].T


def _hgrn_prompt(z3, lb, ng, s0, col0, n_heads):
    b, l, _ = z3.shape
    tl = _pick(l, (512, 256, 128, 64))
    kern = functools.partial(_hgrn_prompt_kernel, tl=tl, chunk=HGRN_CHUNK, sub=HGRN_SUB)
    zspec = lambda k: pl.BlockSpec((None, tl, HEAD_DIM), lambda i, h, t: (i, t, col0 + k * n_heads + h))
    hspec = pl.BlockSpec((1, HEAD_DIM), lambda i, h, t: (0, h))
    sspec = pl.BlockSpec((None, None, HEAD_DIM, HEAD_DIM), lambda i, h, t: (i, h, 0, 0))
    return pl.pallas_call(
        kern,
        grid=(b, n_heads, l // tl),
        in_specs=[zspec(0), zspec(1), zspec(2), zspec(3), hspec, hspec, sspec],
        out_specs=[pl.BlockSpec((None, tl, HEAD_DIM), lambda i, h, t: (i, t, h)), sspec],
        out_shape=[jax.ShapeDtypeStruct((b, l, n_heads * HEAD_DIM), BF16),
                   jax.ShapeDtypeStruct((b, n_heads, HEAD_DIM, HEAD_DIM), F32)],
        scratch_shapes=[pltpu.VMEM((HEAD_DIM, HEAD_DIM), F32)],
        compiler_params=_params("arbitrary", "arbitrary", "arbitrary"),
        name="hgrn_prompt",
    )(z3, z3, z3, z3, lb, ng, s0)


def _hgrn_sample_kernel(z_ref, lb_ref, ng_ref, s0_ref, o_ref, s_out_ref, st_ref,
                        *, l, chunk, col0, n_heads):
    consts = _hgrn_consts(chunk, chunk)
    live = lax.broadcasted_iota(jnp.int32, (chunk, HEAD_DIM), 0) < l
    pad = jnp.zeros((chunk - l, HEAD_DIM), F32)
    for h in range(n_heads):
        col = lambda k: slice((col0 + k * n_heads + h) * HEAD_DIM, (col0 + k * n_heads + h + 1) * HEAD_DIM)
        hs = slice(h * HEAD_DIM, (h + 1) * HEAD_DIM)
        padded = lambda k: jnp.concatenate([z_ref[:, col(k)], pad], axis=0)
        st_ref[...] = s0_ref[h].T
        o = _hgrn_chunk(padded(0), padded(1), padded(2), _lb_terms(lb_ref[:, hs]), st_ref, consts,
                        chunk, chunk, live=live)
        o = _silu(z_ref[:, col(3)]) * _rms_rows(o[0:l], ng_ref[:, hs])
        o_ref[:, hs] = o.astype(BF16)
        s_out_ref[h] = st_ref[...].T


def _hgrn_sample(z3, lb, ng, s0, col0, n_heads):
    b, l, ncols = z3.shape
    chunk = 16
    kern = functools.partial(_hgrn_sample_kernel, l=l, chunk=chunk, col0=col0, n_heads=n_heads)
    d = n_heads * HEAD_DIM
    sspec = pl.BlockSpec((None, n_heads, HEAD_DIM, HEAD_DIM), lambda i: (i, 0, 0, 0))
    return pl.pallas_call(
        kern,
        grid=(b,),
        in_specs=[pl.BlockSpec((None, l, ncols), lambda i: (i, 0, 0)),
                  pl.BlockSpec((1, d), lambda i: (0, 0)),
                  pl.BlockSpec((1, d), lambda i: (0, 0)),
                  sspec],
        out_specs=[pl.BlockSpec((None, l, d), lambda i: (i, 0, 0)), sspec],
        out_shape=[jax.ShapeDtypeStruct((b, l, d), BF16),
                   jax.ShapeDtypeStruct((b, n_heads, HEAD_DIM, HEAD_DIM), F32)],
        scratch_shapes=[pltpu.VMEM((HEAD_DIM, HEAD_DIM), F32)],
        compiler_params=_params("arbitrary"),
        name="hgrn_sample",
    )(z3, lb, ng, s0)


_NT = (((1,), (1,)), ((), ()))


def _topk_select(sc, n_cand):
    lane = lax.broadcasted_iota(jnp.int32, sc.shape, 1)
    rank = jnp.zeros(sc.shape, F32)
    for m in range(n_cand):
        sc_m = sc[:, m:m + 1]
        ge = jnp.where(sc_m >= sc, 1.0, 0.0)
        gt = jnp.where(sc_m > sc, 1.0, 0.0)
        rank = rank + jnp.where(lane > m, ge, gt)
    return jnp.where(rank < MOBA_TOPK, 1.0, 0.0)


def _moba_prompt_kernel(slope_ref, q_ref, k_ref, v_ref, g_ref, qg_ref, kg_ref,
                        o_ref, ko_ref, vo_ref, qn_ref, qb_ref, kb_ref, vb_ref, km_ref, *, l, nb):
    blk = MOBA_BLOCK
    slope = slope_ref[pl.program_id(1)]
    scale = HEAD_DIM ** -0.5
    qn = _rms_rows(q_ref[...], qg_ref[...])
    qn_ref[...] = qn
    qb_ref[...] = qn.astype(BF16)
    kn = _rms_rows(k_ref[...], kg_ref[...])
    kb_ref[...] = kn.astype(BF16)
    ko_ref[...] = kn.reshape(l // PAGE_SIZE, PAGE_SIZE, HEAD_DIM)
    v = v_ref[...]
    vb_ref[...] = v.astype(BF16)
    vo_ref[...] = v.reshape(l // PAGE_SIZE, PAGE_SIZE, HEAD_DIM)
    km_ref[...] = jnp.zeros(km_ref.shape, F32)
    for n in range(nb):
        kblk = ko_ref[n * PAGES_PER_BLOCK:(n + 1) * PAGES_PER_BLOCK].reshape(blk, HEAD_DIM)
        km_ref[n:n + 1, :] = jnp.mean(kblk, axis=0, keepdims=True)
    r = lax.broadcasted_iota(jnp.int32, (blk, blk), 0)
    c = lax.broadcasted_iota(jnp.int32, (blk, blk), 1)
    causal = c <= r
    bias = slope * (r - c).astype(F32)
    for i in range(nb):
        rows = slice(i * blk, (i + 1) * blk)
        nk = (i + 1) * blk
        s = lax.dot_general(qb_ref[rows, :], kb_ref[0:nk, :], _NT, preferred_element_type=F32) * scale
        if i > MOBA_TOPK:
            sc = lax.dot_general(qn_ref[rows, :], km_ref[...], _NT, preferred_element_type=F32,
                                 precision=lax.Precision.HIGHEST)
            sel = _topk_select(sc, i)
        parts = []
        for n in range(i + 1):
            sn = s[:, n * blk:(n + 1) * blk] - (bias + slope * float((i - n) * blk))
            if n == i:
                sn = jnp.where(causal, sn, MASKED)
            elif i > MOBA_TOPK:
                sn = jnp.where(sel[:, n:n + 1] > 0.5, sn, MASKED)
            parts.append(sn)
        s = jnp.concatenate(parts, axis=1) if i > 0 else parts[0]
        m = jnp.max(s, axis=-1, keepdims=True)
        p = jnp.exp(s - m)
        den = jnp.sum(p, axis=-1, keepdims=True)
        o = jnp.dot(p.astype(BF16), vb_ref[0:nk, :], preferred_element_type=F32) / den
        o_ref[rows, :] = (_silu(g_ref[rows, :]) * o).astype(BF16)


def _moba_prompt(z3, qg, kg, slopes, col0, n_heads):
    b, l, _ = z3.shape
    assert l % MOBA_BLOCK == 0
    nb = l // MOBA_BLOCK
    npg = l // PAGE_SIZE
    kern = functools.partial(_moba_prompt_kernel, l=l, nb=nb)
    zspec = lambda k: pl.BlockSpec((None, l, HEAD_DIM), lambda i, h: (i, 0, col0 + k * n_heads + h))
    gspec = pl.BlockSpec((1, HEAD_DIM), lambda i, h: (0, 0))
    pspec = pl.BlockSpec((None, npg, None, PAGE_SIZE, HEAD_DIM), lambda i, h: (i, 0, h, 0, 0))
    page_shape = jax.ShapeDtypeStruct((b, npg, n_heads, PAGE_SIZE, HEAD_DIM), F32)
    return pl.pallas_call(
        kern,
        grid=(b, n_heads),
        in_specs=[pl.BlockSpec(memory_space=pltpu.SMEM),
                  zspec(0), zspec(1), zspec(2), zspec(3), gspec, gspec],
        out_specs=[pl.BlockSpec((None, l, HEAD_DIM), lambda i, h: (i, 0, h)), pspec, pspec],
        out_shape=[jax.ShapeDtypeStruct((b, l, n_heads * HEAD_DIM), BF16), page_shape, page_shape],
        scratch_shapes=[pltpu.VMEM((l, HEAD_DIM), F32),
                        pltpu.VMEM((l, HEAD_DIM), BF16),
                        pltpu.VMEM((l, HEAD_DIM), BF16),
                        pltpu.VMEM((l, HEAD_DIM), BF16),
                        pltpu.VMEM((HEAD_DIM, HEAD_DIM), F32)],
        compiler_params=_params("arbitrary", "arbitrary"),
        name="moba_prompt",
    )(jnp.asarray(slopes, F32), z3, z3, z3, z3, qg, kg)


def _moba_sample_kernel(pt_ref, z_ref, qg_ref, kg_ref, *rest, l, pp, n_heads, nb, col0, past, slopes):
    del pt_ref
    k_refs, v_refs = rest[:pp], rest[pp:2 * pp]
    o_ref, ko_ref, vo_ref = rest[2 * pp:2 * pp + 3]
    qn_ref, m_ref, l_ref, acc_ref, km_ref = rest[2 * pp + 3:]
    blk = MOBA_BLOCK
    scale = HEAD_DIM ** -0.5
    j = pl.program_id(1)
    col = lambda k, h: slice((col0 + k * n_heads + h) * HEAD_DIM, (col0 + k * n_heads + h + 1) * HEAD_DIM)

    @pl.when(j == 0)
    def _():
        for h in range(n_heads):
            qn_ref[h] = _rms_rows(z_ref[:, col(0, h)], qg_ref[...])
            ko_ref[h] = _rms_rows(z_ref[:, col(1, h)], kg_ref[...])
            vo_ref[h] = z_ref[:, col(2, h)]
        km_ref[...] = jnp.zeros(km_ref.shape, F32)

    r = lax.broadcasted_iota(jnp.int32, (l, blk), 0)
    c = lax.broadcasted_iota(jnp.int32, (l, blk), 1)
    rc = (r - c).astype(F32)
    bps = pp // PAGES_PER_BLOCK
    for bi in range(bps):
        n = j * bps + bi
        off = (past - n * blk).astype(F32)
        for h in range(n_heads):
            pages = range(bi * PAGES_PER_BLOCK, (bi + 1) * PAGES_PER_BLOCK)
            kblk = jnp.concatenate([k_refs[p][h] for p in pages], axis=0)
            vblk = jnp.concatenate([v_refs[p][h] for p in pages], axis=0)
            km_ref[h, pl.ds(n, 1), :] = jnp.mean(kblk, axis=0, keepdims=True)
            s = lax.dot_general(qn_ref[h].astype(BF16), kblk.astype(BF16), _NT,
                                preferred_element_type=F32) * scale
            s = s - slopes[h] * (rc + off)
            m = jnp.max(s, axis=-1, keepdims=True)
            p_ = jnp.exp(s - m)
            m_ref[h, n] = jnp.broadcast_to(m, (l, HEAD_DIM))
            l_ref[h, n] = jnp.broadcast_to(jnp.sum(p_, axis=-1, keepdims=True), (l, HEAD_DIM))
            acc_ref[h, n] = jnp.dot(p_.astype(BF16), vblk.astype(BF16), preferred_element_type=F32)

    @pl.when(j == pl.num_programs(1) - 1)
    def _():
        ro = lax.broadcasted_iota(jnp.int32, (l, HEAD_DIM), 0)
        co = lax.broadcasted_iota(jnp.int32, (l, HEAD_DIM), 1)
        zpad = jnp.zeros((HEAD_DIM - l, HEAD_DIM), F32)
        for h in range(n_heads):
            qn = qn_ref[h]
            sc = lax.dot_general(qn, km_ref[h], _NT, preferred_element_type=F32,
                                 precision=lax.Precision.HIGHEST)
            sel = _topk_select(sc, nb)
            kown = jnp.concatenate([ko_ref[h], zpad], axis=0).astype(BF16)
            vown = jnp.concatenate([vo_ref[h], zpad], axis=0).astype(BF16)
            s = lax.dot_general(qn.astype(BF16), kown, _NT, preferred_element_type=F32) * scale
            s = jnp.where(co <= ro, s - slopes[h] * (ro - co).astype(F32), MASKED)
            mx = jnp.broadcast_to(jnp.max(s, axis=-1, keepdims=True), (l, HEAD_DIM))
            keep = [sel[:, n:n + 1] > 0.5 for n in range(nb)]
            for n in range(nb):
                mx = jnp.maximum(mx, jnp.where(keep[n], m_ref[h, n], MASKED))
            p_ = jnp.exp(s - mx)
            den = jnp.broadcast_to(jnp.sum(p_, axis=-1, keepdims=True), (l, HEAD_DIM))
            num = jnp.dot(p_.astype(BF16), vown, preferred_element_type=F32)
            for n in range(nb):
                w = jnp.where(keep[n], jnp.exp(m_ref[h, n] - mx), 0.0)
                den = den + w * l_ref[h, n]
                num = num + w * acc_ref[h, n]
            hs = slice(h * HEAD_DIM, (h + 1) * HEAD_DIM)
            o_ref[:, hs] = (_silu(z_ref[:, col(3, h)]) * (num / den)).astype(BF16)


def _moba_sample(z3, qg, kg, cache_k, cache_v, page_table, layer, slopes, col0, n_heads):
    b, l, ncols = z3.shape
    n_pages = page_table.shape[1]
    assert n_pages % PAGES_PER_BLOCK == 0, "the own block must hold no cached page"
    nb = n_pages // PAGES_PER_BLOCK
    assert nb <= HEAD_DIM
    pp = _pick(n_pages, (8, 4, 2))
    kern = functools.partial(_moba_sample_kernel, l=l, pp=pp, n_heads=n_heads, nb=nb, col0=col0,
                             past=n_pages * PAGE_SIZE, slopes=slopes)
    d = n_heads * HEAD_DIM
    gspec = pl.BlockSpec((1, HEAD_DIM), lambda i, j, pt: (0, 0))

    def page_spec(p):
        return pl.BlockSpec((None, None, n_heads, PAGE_SIZE, HEAD_DIM),
                            lambda i, j, pt: (layer, pt[i, j * pp + p], 0, 0, 0))

    new_spec = pl.BlockSpec((None, n_heads, l, HEAD_DIM), lambda i, j, pt: (i, 0, 0, 0))
    new_shape = jax.ShapeDtypeStruct((b, n_heads, l, HEAD_DIM), F32)
    stat = pltpu.VMEM((n_heads, nb, l, HEAD_DIM), F32)
    return pl.pallas_call(
        kern,
        grid_spec=pltpu.PrefetchScalarGridSpec(
            num_scalar_prefetch=1,
            grid=(b, n_pages // pp),
            in_specs=[pl.BlockSpec((None, l, ncols), lambda i, j, pt: (i, 0, 0)), gspec, gspec]
                     + [page_spec(p) for p in range(pp)] * 2,
            out_specs=[pl.BlockSpec((None, l, d), lambda i, j, pt: (i, 0, 0)), new_spec, new_spec],
            scratch_shapes=[pltpu.VMEM((n_heads, l, HEAD_DIM), F32), stat, stat, stat,
                            pltpu.VMEM((n_heads, HEAD_DIM, HEAD_DIM), F32)]),
        out_shape=[jax.ShapeDtypeStruct((b, l, d), BF16), new_shape, new_shape],
        compiler_params=_params("arbitrary", "arbitrary"),
        name="moba_sample",
    )(page_table, z3, qg, kg, *([cache_k] * pp), *([cache_v] * pp))


def _alibi_slopes(n):
    def geo(m):
        start = 2.0 ** (-8.0 / m)
        return [start ** (i + 1) for i in range(m)]
    if (n & (n - 1)) == 0:
        return geo(n)
    c = 2 ** int(math.floor(math.log2(n)))
    return geo(c) + geo(2 * c)[0::2][: n - c]


def kernel(x_prompt, x_sample, cache_k, cache_v, state_hgrn, state_pool, page_table, norm_gain, w_in,
           pool_w, pool_scale, hgrn_lb, hgrn_norm_gain, q_norm_gain, k_norm_gain, w_out):
    depth, d, _ = w_in.shape
    d_pool = d // 4
    d_hgrn = (d - d_pool) // 2
    d_att = d - d_pool - d_hgrn
    h_hgrn, h_att = d_hgrn // HEAD_DIM, d_att // HEAD_DIM
    col_hgrn = 2 * d_pool // HEAD_DIM
    col_att = col_hgrn + 4 * h_hgrn
    b, t, _ = x_prompt.shape
    sb, sl, _ = x_sample.shape
    past = page_table.shape[1] * PAGE_SIZE
    slopes = [float(np.float32(s)) for s in _alibi_slopes(h_att)]
    lb_cum = jnp.cumsum(jax.nn.softmax(hgrn_lb.astype(F32), axis=0), axis=0)
    lb_all = lb_cum - lb_cum[:1]
    zero_hist = jnp.zeros((b, HIST_ROWS, d_pool), F32)
    zero_state = jnp.zeros((b, h_hgrn, HEAD_DIM, HEAD_DIM), F32)

    y_p = x_prompt.reshape(b * t, d)
    y_s = x_sample.reshape(sb * sl, d)
    outs = [[] for _ in range(8)]
    for layer in range(depth):
        w_i = w_in[layer].astype(BF16)
        w_o = w_out[layer].astype(BF16)
        wa, wb, wc = w_o[:d_pool], w_o[d_pool:d_pool + d_hgrn], w_o[d_pool + d_hgrn:]
        w_p = pool_w[layer].astype(BF16)
        gain = norm_gain[layer][None]
        scale = pool_scale[layer][None]
        lb, ng = lb_all[layer][None], hgrn_norm_gain[layer][None]
        qg, kg = q_norm_gain[layer][None], k_norm_gain[layer][None]

        z = _inproj(y_p, gain, w_i).reshape(b, t, -1)
        oa, hist_p = _pool(z, zero_hist, w_p, scale, 0, d_pool)
        ob, s_p = _hgrn_prompt(z, lb, ng, zero_state, col_hgrn, h_hgrn)
        oc, k_p, v_p = _moba_prompt(z, qg, kg, slopes, col_att, h_att)
        y_p = _outproj(y_p, oa.reshape(b * t, -1), ob.reshape(b * t, -1), oc.reshape(b * t, -1), wa, wb, wc)

        z = _inproj(y_s, gain, w_i).reshape(sb, sl, -1)
        hist16 = jnp.pad(state_pool[layer], ((0, 0), (HIST_ROWS - POOL_HIST, 0), (0, 0)))
        oa, hist_s = _pool(z, hist16, w_p, scale, past, d_pool)
        ob, s_s = _hgrn_sample(z, lb, ng, state_hgrn[layer], col_hgrn, h_hgrn)
        oc, k_s, v_s = _moba_sample(z, qg, kg, cache_k, cache_v, page_table, layer, slopes, col_att, h_att)
        y_s = _outproj(y_s, oa.reshape(sb * sl, -1), ob.reshape(sb * sl, -1), oc.reshape(sb * sl, -1), wa, wb, wc)

        for lst, a in zip(outs, (k_p, v_p, s_p, hist_p[:, 1:], k_s, v_s, s_s, hist_s[:, 1:])):
            lst.append(a)
    return (y_p.reshape(b, t, d), y_s.reshape(sb, sl, d)) + tuple(jnp.stack(o) for o in outs)
```

```python
import functools
import math

import jax
import jax.numpy as jnp
import numpy as np
from jax import lax
from jax.experimental import pallas as pl
from jax.experimental.pallas import tpu as pltpu

F32 = jnp.float32
BF16 = jnp.bfloat16

HEAD_DIM = 128
POOL_WINDOWS = (2, 4, 8, 16)
POOL_HIST = max(POOL_WINDOWS) - 1
HIST_ROWS = POOL_HIST + 1
HGRN_CHUNK = 64
HGRN_SUB = 16
MOBA_BLOCK = 256
MOBA_TOPK = 3
PAGE_SIZE = 128
PAGES_PER_BLOCK = MOBA_BLOCK // PAGE_SIZE
NORM_EPS = 1e-6
MASKED = -1e30
LOG2E = math.log2(math.e)
MXU_DEPTH = 256
VMEM_LIMIT = 48 * 1024 * 1024
_NT = (((1,), (1,)), ((), ()))


def _pick(n, candidates):
    for c in candidates:
        if n % c == 0:
            return c
    raise ValueError(f"no tile for {n}")


def _params(*sem):
    return pltpu.CompilerParams(dimension_semantics=sem, vmem_limit_bytes=VMEM_LIMIT)


def _silu(x):
    return x / (1.0 + jnp.exp(-x))


def _rms_rows(x, g):
    ms = jnp.mean(x * x, axis=-1, keepdims=True)
    return x * lax.rsqrt(ms + NORM_EPS) * g


def _inproj_kernel(x_ref, g_ref, w_ref, z_ref, h_ref):
    @pl.when(pl.program_id(1) == 0)
    def _():
        h_ref[...] = _rms_rows(x_ref[...], g_ref[...]).astype(BF16)
    z_ref[...] = jnp.dot(h_ref[...], w_ref[...], preferred_element_type=F32)


def _inproj(x2d, gain, w):
    m, d = x2d.shape
    n = w.shape[1]
    tm = _pick(m, (1024, 512, 256, 128, 64, 32, 16, 8))
    tn = _pick(n, (1024, 512, 256, 128))
    return pl.pallas_call(
        _inproj_kernel,
        grid=(m // tm, n // tn),
        in_specs=[pl.BlockSpec((tm, d), lambda i, j: (i, 0)),
                  pl.BlockSpec((1, d), lambda i, j: (0, 0)),
                  pl.BlockSpec((d, tn), lambda i, j: (0, j))],
        out_specs=pl.BlockSpec((tm, tn), lambda i, j: (i, j)),
        out_shape=jax.ShapeDtypeStruct((m, n), F32),
        scratch_shapes=[pltpu.VMEM((tm, d), BF16)],
        compiler_params=_params("arbitrary", "arbitrary"),
        name="inproj",
    )(x2d, gain, w)


def _outproj_kernel(x_ref, a_ref, b_ref, c_ref, wa_ref, wb_ref, wc_ref, y_ref):
    acc = jnp.dot(a_ref[...], wa_ref[...], preferred_element_type=F32)
    acc = acc + jnp.dot(b_ref[...], wb_ref[...], preferred_element_type=F32)
    acc = acc + jnp.dot(c_ref[...], wc_ref[...], preferred_element_type=F32)
    y_ref[...] = x_ref[...] + acc


def _outproj(x2d, oa, ob, oc, wa, wb, wc):
    m, d = x2d.shape
    tm = _pick(m, (1024, 512, 256, 128, 64, 32, 16, 8))
    tn = _pick(d, (512, 256, 128))
    row = lambda a: pl.BlockSpec((tm, a.shape[1]), lambda i, j: (i, 0))
    col = lambda w: pl.BlockSpec((w.shape[0], tn), lambda i, j: (0, j))
    return pl.pallas_call(
        _outproj_kernel,
        grid=(m // tm, d // tn),
        in_specs=[pl.BlockSpec((tm, tn), lambda i, j: (i, j)),
                  row(oa), row(ob), row(oc), col(wa), col(wb), col(wc)],
        out_specs=pl.BlockSpec((tm, tn), lambda i, j: (i, j)),
        out_shape=jax.ShapeDtypeStruct((m, d), F32),
        compiler_params=_params("arbitrary", "arbitrary"),
        name="outproj",
    )(x2d, oa, ob, oc, wa, wb, wc)


def _pool_kernel(u_ref, g_ref, hist_ref, w_ref, scale_ref, o_ref, hist_out_ref, ext_ref,
                 *, tl, pos0, group):
    t = pl.program_id(1)

    @pl.when(t == 0)
    def _():
        ext_ref[0:HIST_ROWS, :] = hist_ref[...]

    @pl.when(t > 0)
    def _():
        ext_ref[0:HIST_ROWS, :] = ext_ref[tl:tl + HIST_ROWS, :]

    ext_ref[HIST_ROWS:HIST_ROWS + tl, :] = u_ref[...]
    pos = pos0 + t * tl + lax.broadcasted_iota(jnp.int32, (tl, group), 0)
    for gi, w in enumerate(POOL_WINDOWS):
        cs = slice(gi * group, (gi + 1) * group)
        acc = ext_ref[HIST_ROWS:HIST_ROWS + tl, cs]
        for j in range(1, w):
            acc = acc + ext_ref[HIST_ROWS - j:HIST_ROWS - j + tl, cs]
        cnt = jnp.minimum(w, pos + 1).astype(F32)
        diff = acc / cnt - u_ref[:, cs]
        out = jnp.dot(diff.astype(BF16), w_ref[gi], preferred_element_type=F32) * scale_ref[:, cs]
        o_ref[:, cs] = (_silu(g_ref[:, cs]) * out).astype(BF16)
    hist_out_ref[...] = ext_ref[tl:tl + HIST_ROWS, :]


def _pool(z3, hist16, w_bf16, scale, pos0, d_pool):
    b, l, _ = z3.shape
    group = d_pool // len(POOL_WINDOWS)
    tl = _pick(l, (512, 256, 128, 64, 32, 16, 8))
    kern = functools.partial(_pool_kernel, tl=tl, pos0=pos0, group=group)
    return pl.pallas_call(
        kern,
        grid=(b, l // tl),
        in_specs=[pl.BlockSpec((None, tl, d_pool), lambda i, t: (i, t, 0)),
                  pl.BlockSpec((None, tl, d_pool), lambda i, t: (i, t, 1)),
                  pl.BlockSpec((None, HIST_ROWS, d_pool), lambda i, t: (i, 0, 0)),
                  pl.BlockSpec((len(POOL_WINDOWS), group, group), lambda i, t: (0, 0, 0)),
                  pl.BlockSpec((1, d_pool), lambda i, t: (0, 0))],
        out_specs=[pl.BlockSpec((None, tl, d_pool), lambda i, t: (i, t, 0)),
                   pl.BlockSpec((None, HIST_ROWS, d_pool), lambda i, t: (i, 0, 0))],
        out_shape=[jax.ShapeDtypeStruct((b, l, d_pool), BF16),
                   jax.ShapeDtypeStruct((b, HIST_ROWS, d_pool), F32)],
        scratch_shapes=[pltpu.VMEM((HIST_ROWS + tl, d_pool), F32)],
        compiler_params=_params("arbitrary", "arbitrary"),
        name="pool",
    )(z3, z3, hist16, w_bf16, scale)


def _cumsum_rows(x):
    n = x.shape[0]
    row = lax.broadcasted_iota(jnp.int32, x.shape, 0)
    s = 1
    while s < n:
        x = x + jnp.where(row >= s, pltpu.roll(x, s, axis=0), 0.0)
        s *= 2
    return x


def _hgrn_chunk(q_raw, z, v, lb_terms, st_ref, consts, chunk, sub, live=None):
    log_lb, log1m_lb, one_m_lb = lb_terms
    ones_kk, diag_mask, col_blk = consts
    lsig = jnp.minimum(z, 0.0) - jnp.log1p(jnp.exp(-jnp.abs(z)))
    b_ = log1m_lb + lsig
    logf = jnp.maximum(log_lb, b_) + jnp.log1p(jnp.exp(-jnp.abs(log_lb - b_)))
    k = one_m_lb * (1.0 / (1.0 + jnp.exp(z)))
    if live is not None:
        logf = jnp.where(live, logf, 0.0)
        k = jnp.where(live, k, 0.0)
    q = _silu(q_raw)
    cum = _cumsum_rows(logf)
    last = cum[chunk - 1:chunk, :]
    st = st_ref[...]
    o = lax.dot_general((q * jnp.exp(cum)).astype(BF16), st.astype(BF16),
                        (((1,), (1,)), ((), ())), preferred_element_type=F32)
    nsub = chunk // sub
    vb16 = v.astype(BF16)
    if nsub > 1:
        atts = [jnp.zeros((sub, chunk), F32)]
        for i in range(1, nsub):
            c_i = cum[i * sub - 1:i * sub, :]
            a = q[i * sub:(i + 1) * sub] * jnp.exp(cum[i * sub:(i + 1) * sub] - c_i)
            bmat = k * jnp.exp(jnp.minimum(c_i - cum, 0.0))
            att = lax.dot_general(a.astype(BF16), bmat.astype(BF16),
                                  (((1,), (1,)), ((), ())), preferred_element_type=F32)
            atts.append(jnp.where(col_blk < i, att, 0.0))
        att_all = jnp.concatenate(atts, axis=0)
        o = o + jnp.dot(att_all.astype(BF16), vb16, preferred_element_type=F32)
    diag = []
    for i in range(nsub):
        sl = slice(i * sub, (i + 1) * sub)
        cb, kb, qb, vb = cum[sl], k[sl], q[sl], v[sl]
        ws = []
        for t in range(sub):
            e = jnp.exp(jnp.minimum(cb[t:t + 1, :] - cb, 0.0))
            ws.append((qb[t:t + 1, :] * kb) * e)
        w = jnp.concatenate(ws, axis=0)
        att_rep = jnp.dot(w.astype(BF16), ones_kk, preferred_element_type=F32)
        p = jnp.where(diag_mask, att_rep * jnp.concatenate([vb] * sub, axis=0), 0.0)
        diag.append(jnp.sum(p.reshape(sub, sub, HEAD_DIM), axis=1))
    o = o + (jnp.concatenate(diag, axis=0) if nsub > 1 else diag[0])
    kd = (k * jnp.exp(last - cum)).astype(BF16)
    upd = lax.dot_general(vb16, kd, (((0,), (0,)), ((), ())), preferred_element_type=F32)
    st_ref[...] = st * jnp.exp(last) + upd
    return o


def _hgrn_consts(chunk, sub):
    ones_kk = jnp.ones((HEAD_DIM, HEAD_DIM), BF16)
    r = lax.broadcasted_iota(jnp.int32, (sub * sub, HEAD_DIM), 0)
    diag_mask = (r % sub) <= (r // sub)
    col_blk = lax.broadcasted_iota(jnp.int32, (sub, chunk), 1) // sub
    return ones_kk, diag_mask, col_blk


def _lb_terms(lb):
    return jnp.log(lb), jnp.log1p(-lb), 1.0 - lb


def _hgrn_halves(chunk):
    return [chunk >> (i + 1) for i in range(chunk.bit_length() - 1)]


def _hgrn_level_consts(chunk):
    halves = _hgrn_halves(chunk)
    nl = len(halves)
    m = np.zeros((nl + 1, chunk, chunk), np.float32)
    mask = np.zeros((nl, chunk, chunk), np.float32)
    for li, hz in enumerate(halves):
        for t in range(chunk):
            base = (t // (2 * hz)) * 2 * hz
            mid = base + hz
            if t >= mid:
                m[li, t, mid:t + 1] = 1.0
                mask[li, t, base:mid] = 1.0
            else:
                m[li, t, t + 1:mid] = 1.0
    m[nl] = np.tril(np.ones((chunk, chunk), np.float32))
    m3 = np.zeros(((nl + 1) * chunk, MXU_DEPTH), np.float32)
    m3[:, :3 * chunk] = np.tile(m.reshape((nl + 1) * chunk, chunk), (1, 3))
    return jnp.asarray(m3, BF16), jnp.asarray(mask, F32)


def _hgrn_prompt_kernel(q_ref, f_ref, i_ref, g_ref, lb_ref, ng_ref, s0_ref, m3_ref, mask_ref,
                        o_ref, s_out_ref, st_ref, *, l, chunk, group):
    st_ref[...] = s0_ref[...].T
    log_lb, log1m_lb, one_m_lb = _lb_terms(lb_ref[...])
    ng = ng_ref[...]
    halves = _hgrn_halves(chunk)
    nl = len(halves)
    row = lax.broadcasted_iota(jnp.int32, (chunk, HEAD_DIM), 0)
    uppers = [(row % (2 * hz)) >= hz for hz in halves]
    ones_kk = jnp.ones((HEAD_DIM, HEAD_DIM), BF16)
    kpad = jnp.zeros((MXU_DEPTH - 3 * chunk, HEAD_DIM), BF16)

    def body(c, carry):
        rows = [pl.ds(pl.multiple_of(c * (chunk * group) + gi * chunk, chunk), chunk) for gi in range(group)]
        qs, ks, vs, e2s = [], [], [], []
        for gi in range(group):
            z = f_ref[rows[gi], :]
            lsig = jnp.minimum(z, 0.0) - jnp.log1p(jnp.exp(-jnp.abs(z)))
            b_ = log1m_lb + lsig
            logf = jnp.maximum(log_lb, b_) + jnp.log1p(jnp.exp(-jnp.abs(log_lb - b_)))
            ks.append(one_m_lb * (1.0 / (1.0 + jnp.exp(z))))
            qs.append(_silu(q_ref[rows[gi], :]))
            vs.append(i_ref[rows[gi], :])
            g2 = logf * LOG2E
            hi = g2.astype(BF16)
            r1 = g2 - hi.astype(F32)
            mid = r1.astype(BF16)
            lo = (r1 - mid.astype(F32)).astype(BF16)
            e2s.append(jnp.dot(m3_ref[...], jnp.concatenate([hi, mid, lo, kpad], axis=0),
                               preferred_element_type=F32))
        atts = []
        for gi in range(group):
            att = None
            for li in range(nl):
                p = jnp.exp2(e2s[gi][li * chunk:(li + 1) * chunk])
                ab = (jnp.where(uppers[li], qs[gi], ks[gi]) * p).astype(BF16)
                a = lax.dot_general(ab, ab, _NT, preferred_element_type=F32) * mask_ref[li]
                att = a if att is None else att + a
            atts.append(att)
        outs, kds, qes, dec = [], [], [], []
        for gi in range(group):
            q, k, v = qs[gi], ks[gi], vs[gi]
            cum2 = e2s[gi][nl * chunk:(nl + 1) * chunk]
            last2 = cum2[chunk - 1:chunk, :]
            o = jnp.dot(atts[gi].astype(BF16), v.astype(BF16), preferred_element_type=F32)
            o = o + jnp.dot((q * k).astype(BF16), ones_kk, preferred_element_type=F32) * v
            outs.append(o)
            qes.append((q * jnp.exp2(cum2)).astype(BF16))
            kds.append((k * jnp.exp2(last2 - cum2)).astype(BF16))
            dec.append(jnp.exp2(last2))
        upds = [lax.dot_general(vs[gi].astype(BF16), kds[gi], (((0,), (0,)), ((), ())),
                                preferred_element_type=F32) for gi in range(group)]
        st = st_ref[...]
        for gi in range(group):
            o = outs[gi] + lax.dot_general(qes[gi], st.astype(BF16), _NT, preferred_element_type=F32)
            o_ref[rows[gi], :] = (_silu(g_ref[rows[gi], :]) * _rms_rows(o, ng)).astype(BF16)
            st = st * dec[gi] + upds[gi]
        st_ref[...] = st
        return carry

    lax.fori_loop(0, l // (chunk * group), body, 0)
    s_out_ref[...] = st_ref[...].T


def _hgrn_prompt(z3, lb, ng, s0, col0, n_heads):
    b, l, _ = z3.shape
    chunk = HGRN_CHUNK
    group = _pick(l // chunk, (16, 8, 4, 2, 1))
    m3, mask = _hgrn_level_consts(chunk)
    kern = functools.partial(_hgrn_prompt_kernel, l=l, chunk=chunk, group=group)
    zspec = lambda k: pl.BlockSpec((None, l, HEAD_DIM), lambda i, h: (i, 0, col0 + k * n_heads + h))
    hspec = pl.BlockSpec((1, HEAD_DIM), lambda i, h: (0, h))
    sspec = pl.BlockSpec((None, None, HEAD_DIM, HEAD_DIM), lambda i, h: (i, h, 0, 0))
    return pl.pallas_call(
        kern,
        grid=(b, n_heads),
        in_specs=[zspec(0), zspec(1), zspec(2), zspec(3), hspec, hspec, sspec,
                  pl.BlockSpec(m3.shape, lambda i, h: (0, 0)),
                  pl.BlockSpec(mask.shape, lambda i, h: (0, 0, 0))],
        out_specs=[pl.BlockSpec((None, l, HEAD_DIM), lambda i, h: (i, 0, h)), sspec],
        out_shape=[jax.ShapeDtypeStruct((b, l, n_heads * HEAD_DIM), BF16),
                   jax.ShapeDtypeStruct((b, n_heads, HEAD_DIM, HEAD_DIM), F32)],
        scratch_shapes=[pltpu.VMEM((HEAD_DIM, HEAD_DIM), F32)],
        compiler_params=_params("arbitrary", "arbitrary"),
        name="hgrn_prompt",
    )(z3, z3, z3, z3, lb, ng, s0, m3, mask)


def _hgrn_sample_kernel(z_ref, lb_ref, ng_ref, s0_ref, o_ref, s_out_ref, st_ref,
                        *, l, chunk, col0, n_heads):
    consts = _hgrn_consts(chunk, chunk)
    live = lax.broadcasted_iota(jnp.int32, (chunk, HEAD_DIM), 0) < l
    pad = jnp.zeros((chunk - l, HEAD_DIM), F32)
    for h in range(n_heads):
        col = lambda k: slice((col0 + k * n_heads + h) * HEAD_DIM, (col0 + k * n_heads + h + 1) * HEAD_DIM)
        hs = slice(h * HEAD_DIM, (h + 1) * HEAD_DIM)
        padded = lambda k: jnp.concatenate([z_ref[:, col(k)], pad], axis=0)
        st_ref[...] = s0_ref[h].T
        o = _hgrn_chunk(padded(0), padded(1), padded(2), _lb_terms(lb_ref[:, hs]), st_ref, consts,
                        chunk, chunk, live=live)
        o = _silu(z_ref[:, col(3)]) * _rms_rows(o[0:l], ng_ref[:, hs])
        o_ref[:, hs] = o.astype(BF16)
        s_out_ref[h] = st_ref[...].T


def _hgrn_sample(z3, lb, ng, s0, col0, n_heads):
    b, l, ncols = z3.shape
    chunk = 16
    kern = functools.partial(_hgrn_sample_kernel, l=l, chunk=chunk, col0=col0, n_heads=n_heads)
    d = n_heads * HEAD_DIM
    sspec = pl.BlockSpec((None, n_heads, HEAD_DIM, HEAD_DIM), lambda i: (i, 0, 0, 0))
    return pl.pallas_call(
        kern,
        grid=(b,),
        in_specs=[pl.BlockSpec((None, l, ncols), lambda i: (i, 0, 0)),
                  pl.BlockSpec((1, d), lambda i: (0, 0)),
                  pl.BlockSpec((1, d), lambda i: (0, 0)),
                  sspec],
        out_specs=[pl.BlockSpec((None, l, d), lambda i: (i, 0, 0)), sspec],
        out_shape=[jax.ShapeDtypeStruct((b, l, d), BF16),
                   jax.ShapeDtypeStruct((b, n_heads, HEAD_DIM, HEAD_DIM), F32)],
        scratch_shapes=[pltpu.VMEM((HEAD_DIM, HEAD_DIM), F32)],
        compiler_params=_params("arbitrary"),
        name="hgrn_sample",
    )(z3, lb, ng, s0)


def _topk_select(sc, n_cand):
    lane = lax.broadcasted_iota(jnp.int32, sc.shape, 1)
    rank = jnp.zeros(sc.shape, F32)
    for m in range(n_cand):
        sc_m = sc[:, m:m + 1]
        ge = jnp.where(sc_m >= sc, 1.0, 0.0)
        gt = jnp.where(sc_m > sc, 1.0, 0.0)
        rank = rank + jnp.where(lane > m, ge, gt)
    return jnp.where(rank < MOBA_TOPK, 1.0, 0.0)


def _moba_prompt_kernel(slope_ref, q_ref, k_ref, v_ref, g_ref, qg_ref, kg_ref,
                        o_ref, ko_ref, vo_ref, qn_ref, qb_ref, kb_ref, vb_ref, km_ref, *, l, nb):
    blk = MOBA_BLOCK
    slope2 = slope_ref[pl.program_id(1)] * LOG2E
    qn = _rms_rows(q_ref[...], qg_ref[...])
    qn_ref[...] = qn
    qb_ref[...] = (qn * (HEAD_DIM ** -0.5 * LOG2E)).astype(BF16)
    kn = _rms_rows(k_ref[...], kg_ref[...])
    kb_ref[...] = kn.astype(BF16)
    ko_ref[...] = kn.reshape(l // PAGE_SIZE, PAGE_SIZE, HEAD_DIM)
    v = v_ref[...]
    vb_ref[...] = v.astype(BF16)
    vo_ref[...] = v.reshape(l // PAGE_SIZE, PAGE_SIZE, HEAD_DIM)
    km_ref[...] = jnp.zeros(km_ref.shape, F32)
    for n in range(nb):
        kblk = ko_ref[n * PAGES_PER_BLOCK:(n + 1) * PAGES_PER_BLOCK].reshape(blk, HEAD_DIM)
        km_ref[n:n + 1, :] = jnp.mean(kblk, axis=0, keepdims=True)
    r = lax.broadcasted_iota(jnp.int32, (blk, blk), 0)
    c = lax.broadcasted_iota(jnp.int32, (blk, blk), 1)
    causal = c <= r
    for i in range(nb):
        rows = slice(i * blk, (i + 1) * blk)
        nk = (i + 1) * blk
        kpos = lax.broadcasted_iota(jnp.int32, (1, nk), 1) - i * blk
        s = lax.dot_general(qb_ref[rows, :], kb_ref[0:nk, :], _NT, preferred_element_type=F32)
        s = s + slope2 * kpos.astype(F32)
        if i > MOBA_TOPK:
            sc = lax.dot_general(qn_ref[rows, :], km_ref[...], _NT, preferred_element_type=F32,
                                 precision=lax.Precision.HIGHEST)
            sel = _topk_select(sc, i)
        parts = []
        for n in range(i + 1):
            sn = s[:, n * blk:(n + 1) * blk]
            if n == i:
                sn = jnp.where(causal, sn, MASKED)
            elif i > MOBA_TOPK:
                sn = jnp.where(sel[:, n:n + 1] > 0.5, sn, MASKED)
            parts.append(sn)
        s = jnp.concatenate(parts, axis=1) if i > 0 else parts[0]
        m = jnp.max(s, axis=-1, keepdims=True)
        p = jnp.exp2(s - m)
        den = jnp.sum(p, axis=-1, keepdims=True)
        o = jnp.dot(p.astype(BF16), vb_ref[0:nk, :], preferred_element_type=F32) / den
        o_ref[rows, :] = (_silu(g_ref[rows, :]) * o).astype(BF16)


def _moba_prompt(z3, qg, kg, slopes, col0, n_heads):
    b, l, _ = z3.shape
    assert l % MOBA_BLOCK == 0
    nb = l // MOBA_BLOCK
    npg = l // PAGE_SIZE
    kern = functools.partial(_moba_prompt_kernel, l=l, nb=nb)
    zspec = lambda k: pl.BlockSpec((None, l, HEAD_DIM), lambda i, h: (i, 0, col0 + k * n_heads + h))
    gspec = pl.BlockSpec((1, HEAD_DIM), lambda i, h: (0, 0))
    pspec = pl.BlockSpec((None, npg, None, PAGE_SIZE, HEAD_DIM), lambda i, h: (i, 0, h, 0, 0))
    page_shape = jax.ShapeDtypeStruct((b, npg, n_heads, PAGE_SIZE, HEAD_DIM), F32)
    return pl.pallas_call(
        kern,
        grid=(b, n_heads),
        in_specs=[pl.BlockSpec(memory_space=pltpu.SMEM),
                  zspec(0), zspec(1), zspec(2), zspec(3), gspec, gspec],
        out_specs=[pl.BlockSpec((None, l, HEAD_DIM), lambda i, h: (i, 0, h)), pspec, pspec],
        out_shape=[jax.ShapeDtypeStruct((b, l, n_heads * HEAD_DIM), BF16), page_shape, page_shape],
        scratch_shapes=[pltpu.VMEM((l, HEAD_DIM), F32),
                        pltpu.VMEM((l, HEAD_DIM), BF16),
                        pltpu.VMEM((l, HEAD_DIM), BF16),
                        pltpu.VMEM((l, HEAD_DIM), BF16),
                        pltpu.VMEM((HEAD_DIM, HEAD_DIM), F32)],
        compiler_params=_params("arbitrary", "arbitrary"),
        name="moba_prompt",
    )(jnp.asarray(slopes, F32), z3, z3, z3, z3, qg, kg)


def _moba_sample_kernel(pt_ref, z_ref, qg_ref, kg_ref, *rest, l, pp, n_heads, nb, col0, past, slopes):
    del pt_ref
    k_refs, v_refs = rest[:pp], rest[pp:2 * pp]
    o_ref, ko_ref, vo_ref = rest[2 * pp:2 * pp + 3]
    qn_ref, wq_ref, m_ref, l_ref, acc_ref, km_ref = rest[2 * pp + 3:]
    blk = MOBA_BLOCK
    qscale = HEAD_DIM ** -0.5 * LOG2E
    j = pl.program_id(1)
    col = lambda k, h: slice((col0 + k * n_heads + h) * HEAD_DIM, (col0 + k * n_heads + h + 1) * HEAD_DIM)
    zpad = jnp.zeros((HEAD_DIM - l, HEAD_DIM), F32)
    lane = lax.broadcasted_iota(jnp.int32, (1, HEAD_DIM), 1)
    slope2 = jnp.zeros((1, HEAD_DIM), F32)
    for h in range(n_heads):
        slope2 = jnp.where(lane // l == h, slopes[h] * LOG2E, slope2)
    lq = (lane % l).astype(F32)

    @pl.when(j == 0)
    def _():
        for h in range(n_heads):
            qn = _rms_rows(z_ref[:, col(0, h)], qg_ref[...])
            qn_ref[h] = qn
            qt = jnp.concatenate([qn * qscale, zpad], axis=0).T
            if h:
                qt = pltpu.roll(qt, h * l, axis=1)
            wq_ref[h * HEAD_DIM:(h + 1) * HEAD_DIM, :] = qt.astype(BF16)
            ko_ref[h] = _rms_rows(z_ref[:, col(1, h)], kg_ref[...])
            vo_ref[h] = z_ref[:, col(2, h)]
        km_ref[...] = jnp.zeros(km_ref.shape, F32)
        m_ref[...] = jnp.zeros(m_ref.shape, F32)
        l_ref[...] = jnp.zeros(l_ref.shape, F32)

    key_bias = slope2 * lax.broadcasted_iota(jnp.int32, (blk, HEAD_DIM), 0).astype(F32)
    bps = pp // PAGES_PER_BLOCK
    pages = lambda bi: range(bi * PAGES_PER_BLOCK, (bi + 1) * PAGES_PER_BLOCK)
    scores = []
    for bi in range(bps):
        kcat = jnp.concatenate(
            [jnp.concatenate([k_refs[p][h].astype(BF16) for h in range(n_heads)], axis=1) for p in pages(bi)],
            axis=0)
        scores.append(jnp.dot(kcat, wq_ref[...], preferred_element_type=F32))
    probs = []
    for bi in range(bps):
        n = j * bps + bi
        s = scores[bi] + key_bias + slope2 * ((n * blk - past).astype(F32) - lq)
        m = jnp.max(s, axis=0, keepdims=True)
        p_ = jnp.exp2(s - m)
        m_ref[pl.ds(n, 1), :] = m
        l_ref[pl.ds(n, 1), :] = jnp.sum(p_, axis=0, keepdims=True)
        probs.append(p_.T)
        for h in range(n_heads):
            kblk = jnp.concatenate([k_refs[p][h] for p in pages(bi)], axis=0)
            km_ref[h, pl.ds(n, 1), :] = jnp.mean(kblk, axis=0, keepdims=True)
    for bi in range(bps):
        n = j * bps + bi
        for h in range(n_heads):
            vblk = jnp.concatenate([v_refs[p][h].astype(BF16) for p in pages(bi)], axis=0)
            acc_ref[n, h * l:(h + 1) * l, :] = jnp.dot(probs[bi][h * l:(h + 1) * l, :].astype(BF16), vblk,
                                                       preferred_element_type=F32)

    @pl.when(j == pl.num_programs(1) - 1)
    def _():
        hl = n_heads * l
        ro = lax.broadcasted_iota(jnp.int32, (hl, HEAD_DIM), 0)
        co = lax.broadcasted_iota(jnp.int32, (hl, HEAD_DIM), 1)
        qi = ro % l
        slope_rows = jnp.zeros((hl, HEAD_DIM), F32)
        for h in range(n_heads):
            slope_rows = jnp.where(ro // l == h, slopes[h] * LOG2E, slope_rows)
        sc, s, vown = [], [], []
        for h in range(n_heads):
            qn = qn_ref[h]
            sc.append(lax.dot_general(qn, km_ref[h], _NT, preferred_element_type=F32,
                                      precision=lax.Precision.HIGHEST))
            kown = jnp.concatenate([ko_ref[h], zpad], axis=0).astype(BF16)
            vown.append(jnp.concatenate([vo_ref[h], zpad], axis=0).astype(BF16))
            s.append(lax.dot_general((qn * qscale).astype(BF16), kown, _NT, preferred_element_type=F32))
        keep = (_topk_select(jnp.concatenate(sc, axis=0), nb) > 0.5) & (co < nb)
        mh = m_ref[...].T[0:hl, :]
        lh = l_ref[...].T[0:hl, :]
        s = jnp.concatenate(s, axis=0)
        s = jnp.where(co <= qi, s - slope_rows * (qi - co).astype(F32), MASKED)
        mx = jnp.maximum(jnp.max(s, axis=-1, keepdims=True),
                         jnp.max(jnp.where(keep, mh, MASKED), axis=-1, keepdims=True))
        w = jnp.where(keep, jnp.exp2(mh - mx), 0.0)
        p_ = jnp.exp2(s - mx)
        den = jnp.sum(p_, axis=-1, keepdims=True) + jnp.sum(w * lh, axis=-1, keepdims=True)
        num = jnp.concatenate(
            [jnp.dot(p_[h * l:(h + 1) * l].astype(BF16), vown[h], preferred_element_type=F32)
             for h in range(n_heads)], axis=0)
        for n in range(nb):
            num = num + w[:, n:n + 1] * acc_ref[n]
        o = num / den
        for h in range(n_heads):
            hs = slice(h * HEAD_DIM, (h + 1) * HEAD_DIM)
            o_ref[:, hs] = (_silu(z_ref[:, col(3, h)]) * o[h * l:(h + 1) * l]).astype(BF16)


def _moba_sample(z3, qg, kg, cache_k, cache_v, page_table, layer, slopes, col0, n_heads):
    b, l, ncols = z3.shape
    n_pages = page_table.shape[1]
    assert n_pages % PAGES_PER_BLOCK == 0, "the own block must hold no cached page"
    nb = n_pages // PAGES_PER_BLOCK
    assert nb <= HEAD_DIM
    pp = _pick(n_pages, (8, 4, 2))
    kern = functools.partial(_moba_sample_kernel, l=l, pp=pp, n_heads=n_heads, nb=nb, col0=col0,
                             past=n_pages * PAGE_SIZE, slopes=slopes)
    d = n_heads * HEAD_DIM
    gspec = pl.BlockSpec((1, HEAD_DIM), lambda i, j, pt: (0, 0))

    def page_spec(p):
        return pl.BlockSpec((None, None, n_heads, PAGE_SIZE, HEAD_DIM),
                            lambda i, j, pt: (layer, pt[i, j * pp + p], 0, 0, 0))

    new_spec = pl.BlockSpec((None, n_heads, l, HEAD_DIM), lambda i, j, pt: (i, 0, 0, 0))
    new_shape = jax.ShapeDtypeStruct((b, n_heads, l, HEAD_DIM), F32)
    assert n_heads * l <= HEAD_DIM
    stat = pltpu.VMEM((HEAD_DIM, HEAD_DIM), F32)
    return pl.pallas_call(
        kern,
        grid_spec=pltpu.PrefetchScalarGridSpec(
            num_scalar_prefetch=1,
            grid=(b, n_pages // pp),
            in_specs=[pl.BlockSpec((None, l, ncols), lambda i, j, pt: (i, 0, 0)), gspec, gspec]
                     + [page_spec(p) for p in range(pp)] * 2,
            out_specs=[pl.BlockSpec((None, l, d), lambda i, j, pt: (i, 0, 0)), new_spec, new_spec],
            scratch_shapes=[pltpu.VMEM((n_heads, l, HEAD_DIM), F32),
                            pltpu.VMEM((n_heads * HEAD_DIM, HEAD_DIM), BF16),
                            stat, stat,
                            pltpu.VMEM((nb, n_heads * l, HEAD_DIM), F32),
                            pltpu.VMEM((n_heads, HEAD_DIM, HEAD_DIM), F32)]),
        out_shape=[jax.ShapeDtypeStruct((b, l, d), BF16), new_shape, new_shape],
        compiler_params=_params("arbitrary", "arbitrary"),
        name="moba_sample",
    )(page_table, z3, qg, kg, *([cache_k] * pp), *([cache_v] * pp))


def _alibi_slopes(n):
    def geo(m):
        start = 2.0 ** (-8.0 / m)
        return [start ** (i + 1) for i in range(m)]
    if (n & (n - 1)) == 0:
        return geo(n)
    c = 2 ** int(math.floor(math.log2(n)))
    return geo(c) + geo(2 * c)[0::2][: n - c]


def kernel(x_prompt, x_sample, cache_k, cache_v, state_hgrn, state_pool, page_table, norm_gain, w_in,
           pool_w, pool_scale, hgrn_lb, hgrn_norm_gain, q_norm_gain, k_norm_gain, w_out):
    depth, d, _ = w_in.shape
    d_pool = d // 4
    d_hgrn = (d - d_pool) // 2
    d_att = d - d_pool - d_hgrn
    h_hgrn, h_att = d_hgrn // HEAD_DIM, d_att // HEAD_DIM
    col_hgrn = 2 * d_pool // HEAD_DIM
    col_att = col_hgrn + 4 * h_hgrn
    b, t, _ = x_prompt.shape
    sb, sl, _ = x_sample.shape
    past = page_table.shape[1] * PAGE_SIZE
    slopes = [float(np.float32(s)) for s in _alibi_slopes(h_att)]
    lb_cum = jnp.cumsum(jax.nn.softmax(hgrn_lb.astype(F32), axis=0), axis=0)
    lb_all = lb_cum - lb_cum[:1]
    zero_hist = jnp.zeros((b, HIST_ROWS, d_pool), F32)
    zero_state = jnp.zeros((b, h_hgrn, HEAD_DIM, HEAD_DIM), F32)

    y_p = x_prompt.reshape(b * t, d)
    y_s = x_sample.reshape(sb * sl, d)
    outs = [[] for _ in range(8)]
    for layer in range(depth):
        w_i = w_in[layer].astype(BF16)
        w_o = w_out[layer].astype(BF16)
        wa, wb, wc = w_o[:d_pool], w_o[d_pool:d_pool + d_hgrn], w_o[d_pool + d_hgrn:]
        w_p = pool_w[layer].astype(BF16)
        gain = norm_gain[layer][None]
        scale = pool_scale[layer][None]
        lb, ng = lb_all[layer][None], hgrn_norm_gain[layer][None]
        qg, kg = q_norm_gain[layer][None], k_norm_gain[layer][None]

        z = _inproj(y_p, gain, w_i).reshape(b, t, -1)
        oa, hist_p = _pool(z, zero_hist, w_p, scale, 0, d_pool)
        ob, s_p = _hgrn_prompt(z, lb, ng, zero_state, col_hgrn, h_hgrn)
        oc, k_p, v_p = _moba_prompt(z, qg, kg, slopes, col_att, h_att)
        y_p = _outproj(y_p, oa.reshape(b * t, -1), ob.reshape(b * t, -1), oc.reshape(b * t, -1), wa, wb, wc)

        z = _inproj(y_s, gain, w_i).reshape(sb, sl, -1)
        hist16 = jnp.pad(state_pool[layer], ((0, 0), (HIST_ROWS - POOL_HIST, 0), (0, 0)))
        oa, hist_s = _pool(z, hist16, w_p, scale, past, d_pool)
        ob, s_s = _hgrn_sample(z, lb, ng, state_hgrn[layer], col_hgrn, h_hgrn)
        oc, k_s, v_s = _moba_sample(z, qg, kg, cache_k, cache_v, page_table, layer, slopes, col_att, h_att)
        y_s = _outproj(y_s, oa.reshape(sb * sl, -1), ob.reshape(sb * sl, -1), oc.reshape(sb * sl, -1), wa, wb, wc)

        for lst, a in zip(outs, (k_p, v_p, s_p, hist_p[:, 1:], k_s, v_s, s_s, hist_s[:, 1:])):
            lst.append(a)
    return (y_p.reshape(b, t, d), y_s.reshape(sb, sl, d)) + tuple(jnp.stack(o) for o in outs)
```

```python
import functools
import math

import jax
import jax.numpy as jnp
import numpy as np
from jax import lax
from jax.experimental import pallas as pl
from jax.experimental.pallas import tpu as pltpu

F32 = jnp.float32
BF16 = jnp.bfloat16

HEAD_DIM = 128
POOL_WINDOWS = (2, 4, 8, 16)
POOL_HIST = max(POOL_WINDOWS) - 1
HIST_ROWS = POOL_HIST + 1
HGRN_CHUNK = 64
HGRN_SUB = 16
MOBA_BLOCK = 256
MOBA_TOPK = 3
PAGE_SIZE = 128
PAGES_PER_BLOCK = MOBA_BLOCK // PAGE_SIZE
NORM_EPS = 1e-6
MASKED = -1e30
LOG2E = math.log2(math.e)
MXU_DEPTH = 256
VMEM_LIMIT = 48 * 1024 * 1024
_NT = (((1,), (1,)), ((), ()))


def _pick(n, candidates):
    for c in candidates:
        if n % c == 0:
            return c
    raise ValueError(f"no tile for {n}")


def _params(*sem):
    return pltpu.CompilerParams(dimension_semantics=sem, vmem_limit_bytes=VMEM_LIMIT)


def _silu(x):
    return x / (1.0 + jnp.exp(-x))


def _rms_rows(x, g):
    ms = jnp.mean(x * x, axis=-1, keepdims=True)
    return x * lax.rsqrt(ms + NORM_EPS) * g


def _inproj_kernel(x_ref, g_ref, w_ref, z_ref, h_ref):
    @pl.when(pl.program_id(1) == 0)
    def _():
        h_ref[...] = _rms_rows(x_ref[...], g_ref[...]).astype(BF16)
    z_ref[...] = jnp.dot(h_ref[...], w_ref[...], preferred_element_type=F32)


def _inproj(x2d, gain, w):
    m, d = x2d.shape
    n = w.shape[1]
    tm = _pick(m, (1024, 512, 256, 128, 64, 32, 16, 8))
    tn = _pick(n, (1024, 512, 256, 128))
    return pl.pallas_call(
        _inproj_kernel,
        grid=(m // tm, n // tn),
        in_specs=[pl.BlockSpec((tm, d), lambda i, j: (i, 0)),
                  pl.BlockSpec((1, d), lambda i, j: (0, 0)),
                  pl.BlockSpec((d, tn), lambda i, j: (0, j))],
        out_specs=pl.BlockSpec((tm, tn), lambda i, j: (i, j)),
        out_shape=jax.ShapeDtypeStruct((m, n), F32),
        scratch_shapes=[pltpu.VMEM((tm, d), BF16)],
        compiler_params=_params("arbitrary", "arbitrary"),
        name="inproj",
    )(x2d, gain, w)


def _outproj_kernel(x_ref, a_ref, b_ref, c_ref, *rest, g):
    w_refs, y_ref = rest[:-1], rest[-1]
    acc = None
    wi = 0
    for o_ref in (a_ref, b_ref, c_ref):
        for kb in range(o_ref.shape[1] // g):
            part = jnp.dot(o_ref[:, kb * g:(kb + 1) * g], w_refs[wi][...], preferred_element_type=F32)
            acc = part if acc is None else acc + part
            wi += 1
    y_ref[...] = x_ref[...] + acc


def _outproj(x2d, oa, ob, oc, w):
    m, d = x2d.shape
    widths = (oa.shape[1], ob.shape[1], oc.shape[1])
    g = math.gcd(*widths)
    assert sum(widths) == w.shape[0]
    tm = _pick(m, (1024, 512, 256, 128, 64, 32, 16, 8))
    tn = _pick(d, (512, 256, 128))
    row = lambda a: pl.BlockSpec((tm, a.shape[1]), lambda i, j: (i, 0))
    wspec = lambda kb: pl.BlockSpec((None, g, tn), lambda i, j: (kb, 0, j))
    nkb = w.shape[0] // g
    return pl.pallas_call(
        functools.partial(_outproj_kernel, g=g),
        grid=(m // tm, d // tn),
        in_specs=[pl.BlockSpec((tm, tn), lambda i, j: (i, j)), row(oa), row(ob), row(oc)]
                 + [wspec(kb) for kb in range(nkb)],
        out_specs=pl.BlockSpec((tm, tn), lambda i, j: (i, j)),
        out_shape=jax.ShapeDtypeStruct((m, d), F32),
        compiler_params=_params("arbitrary", "arbitrary"),
        name="outproj",
    )(x2d, oa, ob, oc, *([w.reshape(nkb, g, d)] * nkb))


def _pool_kernel(u_ref, g_ref, hist_ref, w_ref, scale_ref, o_ref, hist_out_ref, ext_ref,
                 *, tl, pos0, group):
    t = pl.program_id(1)

    @pl.when(t == 0)
    def _():
        ext_ref[0:HIST_ROWS, :] = hist_ref[...]

    @pl.when(t > 0)
    def _():
        ext_ref[0:HIST_ROWS, :] = ext_ref[tl:tl + HIST_ROWS, :]

    ext_ref[HIST_ROWS:HIST_ROWS + tl, :] = u_ref[...]
    pos = pos0 + t * tl + lax.broadcasted_iota(jnp.int32, (tl, group), 0)
    for gi, w in enumerate(POOL_WINDOWS):
        cs = slice(gi * group, (gi + 1) * group)
        acc = ext_ref[HIST_ROWS:HIST_ROWS + tl, cs]
        for j in range(1, w):
            acc = acc + ext_ref[HIST_ROWS - j:HIST_ROWS - j + tl, cs]
        cnt = jnp.minimum(w, pos + 1).astype(F32)
        diff = acc / cnt - u_ref[:, cs]
        out = jnp.dot(diff.astype(BF16), w_ref[gi], preferred_element_type=F32) * scale_ref[:, cs]
        o_ref[:, cs] = (_silu(g_ref[:, cs]) * out).astype(BF16)
    hist_out_ref[...] = ext_ref[tl:tl + HIST_ROWS, :]


def _pool(z3, hist16, w_bf16, scale, pos0, d_pool):
    b, l, _ = z3.shape
    group = d_pool // len(POOL_WINDOWS)
    tl = _pick(l, (512, 256, 128, 64, 32, 16, 8))
    kern = functools.partial(_pool_kernel, tl=tl, pos0=pos0, group=group)
    return pl.pallas_call(
        kern,
        grid=(b, l // tl),
        in_specs=[pl.BlockSpec((None, tl, d_pool), lambda i, t: (i, t, 0)),
                  pl.BlockSpec((None, tl, d_pool), lambda i, t: (i, t, 1)),
                  pl.BlockSpec((None, HIST_ROWS, d_pool), lambda i, t: (i, 0, 0)),
                  pl.BlockSpec((len(POOL_WINDOWS), group, group), lambda i, t: (0, 0, 0)),
                  pl.BlockSpec((1, d_pool), lambda i, t: (0, 0))],
        out_specs=[pl.BlockSpec((None, tl, d_pool), lambda i, t: (i, t, 0)),
                   pl.BlockSpec((None, HIST_ROWS, d_pool), lambda i, t: (i, 0, 0))],
        out_shape=[jax.ShapeDtypeStruct((b, l, d_pool), BF16),
                   jax.ShapeDtypeStruct((b, HIST_ROWS, d_pool), F32)],
        scratch_shapes=[pltpu.VMEM((HIST_ROWS + tl, d_pool), F32)],
        compiler_params=_params("arbitrary", "arbitrary"),
        name="pool",
    )(z3, z3, hist16, w_bf16, scale)


def _cumsum_rows(x):
    n = x.shape[0]
    row = lax.broadcasted_iota(jnp.int32, x.shape, 0)
    s = 1
    while s < n:
        x = x + jnp.where(row >= s, pltpu.roll(x, s, axis=0), 0.0)
        s *= 2
    return x


def _hgrn_chunk(q_raw, z, v, lb_terms, st_ref, consts, chunk, sub, live=None):
    log_lb, log1m_lb, one_m_lb = lb_terms
    ones_kk, diag_mask, col_blk = consts
    lsig = jnp.minimum(z, 0.0) - jnp.log1p(jnp.exp(-jnp.abs(z)))
    b_ = log1m_lb + lsig
    logf = jnp.maximum(log_lb, b_) + jnp.log1p(jnp.exp(-jnp.abs(log_lb - b_)))
    k = one_m_lb * (1.0 / (1.0 + jnp.exp(z)))
    if live is not None:
        logf = jnp.where(live, logf, 0.0)
        k = jnp.where(live, k, 0.0)
    q = _silu(q_raw)
    cum = _cumsum_rows(logf)
    last = cum[chunk - 1:chunk, :]
    st = st_ref[...]
    o = lax.dot_general((q * jnp.exp(cum)).astype(BF16), st.astype(BF16),
                        (((1,), (1,)), ((), ())), preferred_element_type=F32)
    nsub = chunk // sub
    vb16 = v.astype(BF16)
    if nsub > 1:
        atts = [jnp.zeros((sub, chunk), F32)]
        for i in range(1, nsub):
            c_i = cum[i * sub - 1:i * sub, :]
            a = q[i * sub:(i + 1) * sub] * jnp.exp(cum[i * sub:(i + 1) * sub] - c_i)
            bmat = k * jnp.exp(jnp.minimum(c_i - cum, 0.0))
            att = lax.dot_general(a.astype(BF16), bmat.astype(BF16),
                                  (((1,), (1,)), ((), ())), preferred_element_type=F32)
            atts.append(jnp.where(col_blk < i, att, 0.0))
        att_all = jnp.concatenate(atts, axis=0)
        o = o + jnp.dot(att_all.astype(BF16), vb16, preferred_element_type=F32)
    diag = []
    for i in range(nsub):
        sl = slice(i * sub, (i + 1) * sub)
        cb, kb, qb, vb = cum[sl], k[sl], q[sl], v[sl]
        ws = []
        for t in range(sub):
            e = jnp.exp(jnp.minimum(cb[t:t + 1, :] - cb, 0.0))
            ws.append((qb[t:t + 1, :] * kb) * e)
        w = jnp.concatenate(ws, axis=0)
        att_rep = jnp.dot(w.astype(BF16), ones_kk, preferred_element_type=F32)
        p = jnp.where(diag_mask, att_rep * jnp.concatenate([vb] * sub, axis=0), 0.0)
        diag.append(jnp.sum(p.reshape(sub, sub, HEAD_DIM), axis=1))
    o = o + (jnp.concatenate(diag, axis=0) if nsub > 1 else diag[0])
    kd = (k * jnp.exp(last - cum)).astype(BF16)
    upd = lax.dot_general(vb16, kd, (((0,), (0,)), ((), ())), preferred_element_type=F32)
    st_ref[...] = st * jnp.exp(last) + upd
    return o


def _hgrn_consts(chunk, sub):
    ones_kk = jnp.ones((HEAD_DIM, HEAD_DIM), BF16)
    r = lax.broadcasted_iota(jnp.int32, (sub * sub, HEAD_DIM), 0)
    diag_mask = (r % sub) <= (r // sub)
    col_blk = lax.broadcasted_iota(jnp.int32, (sub, chunk), 1) // sub
    return ones_kk, diag_mask, col_blk


def _lb_terms(lb):
    return jnp.log(lb), jnp.log1p(-lb), 1.0 - lb


def _hgrn_halves(chunk):
    return [chunk >> (i + 1) for i in range(chunk.bit_length() - 1)]


def _hgrn_level_consts(chunk):
    halves = _hgrn_halves(chunk)
    nl = len(halves)
    m = np.zeros((nl + 1, chunk, chunk), np.float32)
    mask = np.zeros((nl, chunk, chunk), np.float32)
    for li, hz in enumerate(halves):
        for t in range(chunk):
            base = (t // (2 * hz)) * 2 * hz
            mid = base + hz
            if t >= mid:
                m[li, t, mid:t + 1] = 1.0
                mask[li, t, base:mid] = 1.0
            else:
                m[li, t, t + 1:mid] = 1.0
    m[nl] = np.tril(np.ones((chunk, chunk), np.float32))
    m3 = np.zeros(((nl + 1) * chunk, MXU_DEPTH), np.float32)
    m3[:, :3 * chunk] = np.tile(m.reshape((nl + 1) * chunk, chunk), (1, 3))
    return jnp.asarray(m3, BF16), jnp.asarray(mask, F32)


def _hgrn_prompt_kernel(q_ref, f_ref, i_ref, g_ref, lb_ref, ng_ref, s0_ref, m3_ref, mask_ref,
                        o_ref, s_out_ref, st_ref, *, l, chunk, group):
    st_ref[...] = s0_ref[...].T
    log_lb, log1m_lb, one_m_lb = _lb_terms(lb_ref[...])
    ng = ng_ref[...]
    halves = _hgrn_halves(chunk)
    nl = len(halves)
    row = lax.broadcasted_iota(jnp.int32, (chunk, HEAD_DIM), 0)
    uppers = [(row % (2 * hz)) >= hz for hz in halves]
    ones_kk = jnp.ones((HEAD_DIM, HEAD_DIM), BF16)
    kpad = jnp.zeros((MXU_DEPTH - 3 * chunk, HEAD_DIM), BF16)

    def body(c, carry):
        rows = [pl.ds(pl.multiple_of(c * (chunk * group) + gi * chunk, chunk), chunk) for gi in range(group)]
        qs, ks, vs, e2s = [], [], [], []
        for gi in range(group):
            z = f_ref[rows[gi], :]
            lsig = jnp.minimum(z, 0.0) - jnp.log1p(jnp.exp(-jnp.abs(z)))
            b_ = log1m_lb + lsig
            logf = jnp.maximum(log_lb, b_) + jnp.log1p(jnp.exp(-jnp.abs(log_lb - b_)))
            ks.append(one_m_lb * (1.0 / (1.0 + jnp.exp(z))))
            qs.append(_silu(q_ref[rows[gi], :]))
            vs.append(i_ref[rows[gi], :])
            g2 = logf * LOG2E
            hi = g2.astype(BF16)
            r1 = g2 - hi.astype(F32)
            mid = r1.astype(BF16)
            lo = (r1 - mid.astype(F32)).astype(BF16)
            e2s.append(jnp.dot(m3_ref[...], jnp.concatenate([hi, mid, lo, kpad], axis=0),
                               preferred_element_type=F32))
        atts = []
        for gi in range(group):
            att = None
            for li in range(nl):
                p = jnp.exp2(e2s[gi][li * chunk:(li + 1) * chunk])
                ab = (jnp.where(uppers[li], qs[gi], ks[gi]) * p).astype(BF16)
                a = lax.dot_general(ab, ab, _NT, preferred_element_type=F32) * mask_ref[li]
                att = a if att is None else att + a
            atts.append(att)
        outs, kds, qes, dec = [], [], [], []
        for gi in range(group):
            q, k, v = qs[gi], ks[gi], vs[gi]
            cum2 = e2s[gi][nl * chunk:(nl + 1) * chunk]
            last2 = cum2[chunk - 1:chunk, :]
            o = jnp.dot(atts[gi].astype(BF16), v.astype(BF16), preferred_element_type=F32)
            o = o + jnp.dot((q * k).astype(BF16), ones_kk, preferred_element_type=F32) * v
            outs.append(o)
            qes.append((q * jnp.exp2(cum2)).astype(BF16))
            kds.append((k * jnp.exp2(last2 - cum2)).astype(BF16))
            dec.append(jnp.exp2(last2))
        upds = [lax.dot_general(vs[gi].astype(BF16), kds[gi], (((0,), (0,)), ((), ())),
                                preferred_element_type=F32) for gi in range(group)]
        st = st_ref[...]
        for gi in range(group):
            o = outs[gi] + lax.dot_general(qes[gi], st.astype(BF16), _NT, preferred_element_type=F32)
            o_ref[rows[gi], :] = (_silu(g_ref[rows[gi], :]) * _rms_rows(o, ng)).astype(BF16)
            st = st * dec[gi] + upds[gi]
        st_ref[...] = st
        return carry

    lax.fori_loop(0, l // (chunk * group), body, 0)
    s_out_ref[...] = st_ref[...].T


def _hgrn_prompt(z3, lb, ng, s0, col0, n_heads):
    b, l, _ = z3.shape
    chunk = HGRN_CHUNK
    group = _pick(l // chunk, (16, 8, 4, 2, 1))
    m3, mask = _hgrn_level_consts(chunk)
    kern = functools.partial(_hgrn_prompt_kernel, l=l, chunk=chunk, group=group)
    zspec = lambda k: pl.BlockSpec((None, l, HEAD_DIM), lambda i, h: (i, 0, col0 + k * n_heads + h))
    hspec = pl.BlockSpec((1, HEAD_DIM), lambda i, h: (0, h))
    sspec = pl.BlockSpec((None, None, HEAD_DIM, HEAD_DIM), lambda i, h: (i, h, 0, 0))
    return pl.pallas_call(
        kern,
        grid=(b, n_heads),
        in_specs=[zspec(0), zspec(1), zspec(2), zspec(3), hspec, hspec, sspec,
                  pl.BlockSpec(m3.shape, lambda i, h: (0, 0)),
                  pl.BlockSpec(mask.shape, lambda i, h: (0, 0, 0))],
        out_specs=[pl.BlockSpec((None, l, HEAD_DIM), lambda i, h: (i, 0, h)), sspec],
        out_shape=[jax.ShapeDtypeStruct((b, l, n_heads * HEAD_DIM), BF16),
                   jax.ShapeDtypeStruct((b, n_heads, HEAD_DIM, HEAD_DIM), F32)],
        scratch_shapes=[pltpu.VMEM((HEAD_DIM, HEAD_DIM), F32)],
        compiler_params=_params("arbitrary", "arbitrary"),
        name="hgrn_prompt",
    )(z3, z3, z3, z3, lb, ng, s0, m3, mask)


def _hgrn_sample_kernel(z_ref, lb_ref, ng_ref, s0_ref, o_ref, s_out_ref, st_ref,
                        *, l, chunk, col0, n_heads):
    consts = _hgrn_consts(chunk, chunk)
    live = lax.broadcasted_iota(jnp.int32, (chunk, HEAD_DIM), 0) < l
    pad = jnp.zeros((chunk - l, HEAD_DIM), F32)
    for h in range(n_heads):
        col = lambda k: slice((col0 + k * n_heads + h) * HEAD_DIM, (col0 + k * n_heads + h + 1) * HEAD_DIM)
        hs = slice(h * HEAD_DIM, (h + 1) * HEAD_DIM)
        padded = lambda k: jnp.concatenate([z_ref[:, col(k)], pad], axis=0)
        st_ref[...] = s0_ref[h].T
        o = _hgrn_chunk(padded(0), padded(1), padded(2), _lb_terms(lb_ref[:, hs]), st_ref, consts,
                        chunk, chunk, live=live)
        o = _silu(z_ref[:, col(3)]) * _rms_rows(o[0:l], ng_ref[:, hs])
        o_ref[:, hs] = o.astype(BF16)
        s_out_ref[h] = st_ref[...].T


def _hgrn_sample(z3, lb, ng, s0, col0, n_heads):
    b, l, ncols = z3.shape
    chunk = 16
    kern = functools.partial(_hgrn_sample_kernel, l=l, chunk=chunk, col0=col0, n_heads=n_heads)
    d = n_heads * HEAD_DIM
    sspec = pl.BlockSpec((None, n_heads, HEAD_DIM, HEAD_DIM), lambda i: (i, 0, 0, 0))
    return pl.pallas_call(
        kern,
        grid=(b,),
        in_specs=[pl.BlockSpec((None, l, ncols), lambda i: (i, 0, 0)),
                  pl.BlockSpec((1, d), lambda i: (0, 0)),
                  pl.BlockSpec((1, d), lambda i: (0, 0)),
                  sspec],
        out_specs=[pl.BlockSpec((None, l, d), lambda i: (i, 0, 0)), sspec],
        out_shape=[jax.ShapeDtypeStruct((b, l, d), BF16),
                   jax.ShapeDtypeStruct((b, n_heads, HEAD_DIM, HEAD_DIM), F32)],
        scratch_shapes=[pltpu.VMEM((HEAD_DIM, HEAD_DIM), F32)],
        compiler_params=_params("arbitrary"),
        name="hgrn_sample",
    )(z3, lb, ng, s0)


def _topk_select(sc, n_cand):
    lane = lax.broadcasted_iota(jnp.int32, sc.shape, 1)
    rank = jnp.zeros(sc.shape, F32)
    for m in range(n_cand):
        sc_m = sc[:, m:m + 1]
        ge = jnp.where(sc_m >= sc, 1.0, 0.0)
        gt = jnp.where(sc_m > sc, 1.0, 0.0)
        rank = rank + jnp.where(lane > m, ge, gt)
    return jnp.where(rank < MOBA_TOPK, 1.0, 0.0)


def _moba_prompt_kernel(slope_ref, q_ref, k_ref, v_ref, g_ref, qg_ref, kg_ref, *rest, l, nb, n_prev):
    if n_prev:
        kprev_ref, vprev_ref = rest[:2]
        rest = rest[2:]
    o_ref, ko_ref, vo_ref, qn_ref, qa_ref, ka_ref, vb_ref, km_ref = rest
    blk, d = MOBA_BLOCK, HEAD_DIM
    nbp = -(-nb // 8) * 8
    slope2 = slope_ref[pl.program_id(1)] * LOG2E
    if n_prev:
        ko_ref[0:n_prev] = kprev_ref[...]
        vo_ref[0:n_prev] = vprev_ref[...]
    qn = _rms_rows(q_ref[...], qg_ref[...])
    qn_ref[...] = qn
    qa_ref[:, 0:d] = (qn * (d ** -0.5 * LOG2E)).astype(BF16)
    kn = _rms_rows(k_ref[...], kg_ref[...])
    ka_ref[:, 0:d] = kn.astype(BF16)
    key_blk = lax.broadcasted_iota(jnp.int32, (l, d), 0) // blk
    ka_ref[:, d:2 * d] = jnp.where(key_blk == lax.broadcasted_iota(jnp.int32, (l, d), 1), 1.0, 0.0).astype(BF16)
    ko_ref[n_prev] = kn.reshape(l // PAGE_SIZE, PAGE_SIZE, d)
    v = v_ref[...]
    vb_ref[...] = v.astype(BF16)
    vo_ref[n_prev] = v.reshape(l // PAGE_SIZE, PAGE_SIZE, d)
    km_ref[...] = jnp.zeros(km_ref.shape, F32)
    for n in range(nb):
        kblk = ko_ref[n_prev, n * PAGES_PER_BLOCK:(n + 1) * PAGES_PER_BLOCK].reshape(blk, d)
        km_ref[n:n + 1, :] = jnp.mean(kblk, axis=0, keepdims=True)
    causal = lax.broadcasted_iota(jnp.int32, (blk, blk), 1) <= lax.broadcasted_iota(jnp.int32, (blk, blk), 0)
    blk_row = lax.broadcasted_iota(jnp.int32, (nbp, blk), 0)

    def scores(i):
        rows = slice(i * blk, (i + 1) * blk)
        nk = (i + 1) * blk
        if i <= MOBA_TOPK:
            return lax.dot_general(qa_ref[rows, 0:d], ka_ref[0:nk, 0:d], _NT, preferred_element_type=F32)
        sct = lax.dot_general(km_ref[0:nbp, :], qn_ref[rows, :], _NT, preferred_element_type=F32,
                              precision=lax.Precision.HIGHEST)
        rank = jnp.zeros((nbp, blk), F32)
        for m in range(i):
            sm = sct[m:m + 1, :]
            ge = jnp.where(sm >= sct, 1.0, 0.0)
            gt = jnp.where(sm > sct, 1.0, 0.0)
            rank = rank + jnp.where(blk_row > m, ge, gt)
        drop = jnp.where((rank < MOBA_TOPK) | (blk_row >= i), 0.0, MASKED)
        drop = jnp.concatenate([drop, jnp.zeros((d - nbp, blk), F32)], axis=0).T
        qa_ref[rows, d:2 * d] = drop.astype(BF16)
        return lax.dot_general(qa_ref[rows, :], ka_ref[0:nk, :], _NT, preferred_element_type=F32)

    s_next = scores(0)
    for i in range(nb):
        rows = slice(i * blk, (i + 1) * blk)
        n_past = i * blk
        s = s_next
        if i + 1 < nb:
            s_next = scores(i + 1)
        kpos = lax.broadcasted_iota(jnp.int32, (1, n_past + blk), 1) - n_past
        s = s + slope2 * kpos.astype(F32)
        s_own = jnp.where(causal, s[:, n_past:], MASKED)
        m = jnp.max(s_own, axis=-1, keepdims=True)
        if i:
            s_past = s[:, :n_past]
            m = jnp.maximum(m, jnp.max(s_past, axis=-1, keepdims=True))
        p_own = jnp.exp2(s_own - m)
        den = jnp.sum(p_own, axis=-1, keepdims=True)
        o = jnp.dot(p_own.astype(BF16), vb_ref[n_past:n_past + blk, :], preferred_element_type=F32)
        if i:
            p_past = jnp.exp2(s_past - m)
            den = den + jnp.sum(p_past, axis=-1, keepdims=True)
            o = o + jnp.dot(p_past.astype(BF16), vb_ref[0:n_past, :], preferred_element_type=F32)
        o_ref[rows, :] = (_silu(g_ref[rows, :]) * (o / den)).astype(BF16)


def _moba_prompt(z3, qg, kg, slopes, col0, n_heads, prev=None):
    b, l, _ = z3.shape
    assert l % MOBA_BLOCK == 0
    nb = l // MOBA_BLOCK
    assert nb <= HEAD_DIM
    npg = l // PAGE_SIZE
    n_prev = 0 if prev is None else prev[0].shape[0]
    kern = functools.partial(_moba_prompt_kernel, l=l, nb=nb, n_prev=n_prev)
    zspec = lambda k: pl.BlockSpec((None, l, HEAD_DIM), lambda i, h: (i, 0, col0 + k * n_heads + h))
    gspec = pl.BlockSpec((1, HEAD_DIM), lambda i, h: (0, 0))
    pspec = lambda n: pl.BlockSpec((n, None, npg, None, PAGE_SIZE, HEAD_DIM), lambda i, h: (0, i, 0, h, 0, 0))
    page_shape = jax.ShapeDtypeStruct((n_prev + 1, b, npg, n_heads, PAGE_SIZE, HEAD_DIM), F32)
    prev_specs = [pspec(n_prev)] * 2 if n_prev else []
    return pl.pallas_call(
        kern,
        grid=(b, n_heads),
        in_specs=[pl.BlockSpec(memory_space=pltpu.SMEM),
                  zspec(0), zspec(1), zspec(2), zspec(3), gspec, gspec] + prev_specs,
        out_specs=[pl.BlockSpec((None, l, HEAD_DIM), lambda i, h: (i, 0, h)), pspec(n_prev + 1), pspec(n_prev + 1)],
        out_shape=[jax.ShapeDtypeStruct((b, l, n_heads * HEAD_DIM), BF16), page_shape, page_shape],
        scratch_shapes=[pltpu.VMEM((l, HEAD_DIM), F32),
                        pltpu.VMEM((l, 2 * HEAD_DIM), BF16),
                        pltpu.VMEM((l, 2 * HEAD_DIM), BF16),
                        pltpu.VMEM((l, HEAD_DIM), BF16),
                        pltpu.VMEM((HEAD_DIM, HEAD_DIM), F32)],
        compiler_params=_params("arbitrary", "arbitrary"),
        name="moba_prompt",
    )(jnp.asarray(slopes, F32), z3, z3, z3, z3, qg, kg, *(prev or ()))


def _moba_sample_kernel(pt_ref, z_ref, qg_ref, kg_ref, *rest, l, pp, n_heads, nb, col0, past, slopes):
    del pt_ref
    k_refs, v_refs = rest[:pp], rest[pp:2 * pp]
    o_ref, ko_ref, vo_ref = rest[2 * pp:2 * pp + 3]
    qn_ref, wq_ref, m_ref, l_ref, acc_ref, km_ref = rest[2 * pp + 3:]
    blk = MOBA_BLOCK
    qscale = HEAD_DIM ** -0.5 * LOG2E
    j = pl.program_id(1)
    col = lambda k, h: slice((col0 + k * n_heads + h) * HEAD_DIM, (col0 + k * n_heads + h + 1) * HEAD_DIM)
    zpad = jnp.zeros((HEAD_DIM - l, HEAD_DIM), F32)
    lane = lax.broadcasted_iota(jnp.int32, (1, HEAD_DIM), 1)
    slope2 = jnp.zeros((1, HEAD_DIM), F32)
    for h in range(n_heads):
        slope2 = jnp.where(lane // l == h, slopes[h] * LOG2E, slope2)
    lq = (lane % l).astype(F32)

    @pl.when(j == 0)
    def _():
        for h in range(n_heads):
            qn = _rms_rows(z_ref[:, col(0, h)], qg_ref[...])
            qn_ref[h] = qn
            qt = jnp.concatenate([qn * qscale, zpad], axis=0).T
            if h:
                qt = pltpu.roll(qt, h * l, axis=1)
            wq_ref[h * HEAD_DIM:(h + 1) * HEAD_DIM, :] = qt.astype(BF16)
            ko_ref[h] = _rms_rows(z_ref[:, col(1, h)], kg_ref[...])
            vo_ref[h] = z_ref[:, col(2, h)]
        km_ref[...] = jnp.zeros(km_ref.shape, F32)
        m_ref[...] = jnp.zeros(m_ref.shape, F32)
        l_ref[...] = jnp.zeros(l_ref.shape, F32)

    key_bias = slope2 * lax.broadcasted_iota(jnp.int32, (blk, HEAD_DIM), 0).astype(F32)
    bps = pp // PAGES_PER_BLOCK
    pages = lambda bi: range(bi * PAGES_PER_BLOCK, (bi + 1) * PAGES_PER_BLOCK)
    scores = []
    for bi in range(bps):
        kcat = jnp.concatenate(
            [jnp.concatenate([k_refs[p][h].astype(BF16) for h in range(n_heads)], axis=1) for p in pages(bi)],
            axis=0)
        scores.append(jnp.dot(kcat, wq_ref[...], preferred_element_type=F32))
    probs = []
    for bi in range(bps):
        n = j * bps + bi
        s = scores[bi] + key_bias + slope2 * ((n * blk - past).astype(F32) - lq)
        m = jnp.max(s, axis=0, keepdims=True)
        p_ = jnp.exp2(s - m)
        m_ref[pl.ds(n, 1), :] = m
        l_ref[pl.ds(n, 1), :] = jnp.sum(p_, axis=0, keepdims=True)
        probs.append(p_.T)
        for h in range(n_heads):
            kblk = jnp.concatenate([k_refs[p][h] for p in pages(bi)], axis=0)
            km_ref[h, pl.ds(n, 1), :] = jnp.mean(kblk, axis=0, keepdims=True)
    for bi in range(bps):
        n = j * bps + bi
        for h in range(n_heads):
            vblk = jnp.concatenate([v_refs[p][h].astype(BF16) for p in pages(bi)], axis=0)
            acc_ref[n, h * l:(h + 1) * l, :] = jnp.dot(probs[bi][h * l:(h + 1) * l, :].astype(BF16), vblk,
                                                       preferred_element_type=F32)

    @pl.when(j == pl.num_programs(1) - 1)
    def _():
        hl = n_heads * l
        ro = lax.broadcasted_iota(jnp.int32, (hl, HEAD_DIM), 0)
        co = lax.broadcasted_iota(jnp.int32, (hl, HEAD_DIM), 1)
        qi = ro % l
        slope_rows = jnp.zeros((hl, HEAD_DIM), F32)
        for h in range(n_heads):
            slope_rows = jnp.where(ro // l == h, slopes[h] * LOG2E, slope_rows)
        sc, s, vown = [], [], []
        for h in range(n_heads):
            qn = qn_ref[h]
            sc.append(lax.dot_general(qn, km_ref[h], _NT, preferred_element_type=F32,
                                      precision=lax.Precision.HIGHEST))
            kown = jnp.concatenate([ko_ref[h], zpad], axis=0).astype(BF16)
            vown.append(jnp.concatenate([vo_ref[h], zpad], axis=0).astype(BF16))
            s.append(lax.dot_general((qn * qscale).astype(BF16), kown, _NT, preferred_element_type=F32))
        keep = (_topk_select(jnp.concatenate(sc, axis=0), nb) > 0.5) & (co < nb)
        mh = m_ref[...].T[0:hl, :]
        lh = l_ref[...].T[0:hl, :]
        s = jnp.concatenate(s, axis=0)
        s = jnp.where(co <= qi, s - slope_rows * (qi - co).astype(F32), MASKED)
        mx = jnp.maximum(jnp.max(s, axis=-1, keepdims=True),
                         jnp.max(jnp.where(keep, mh, MASKED), axis=-1, keepdims=True))
        w = jnp.where(keep, jnp.exp2(mh - mx), 0.0)
        p_ = jnp.exp2(s - mx)
        den = jnp.sum(p_, axis=-1, keepdims=True) + jnp.sum(w * lh, axis=-1, keepdims=True)
        num = jnp.concatenate(
            [jnp.dot(p_[h * l:(h + 1) * l].astype(BF16), vown[h], preferred_element_type=F32)
             for h in range(n_heads)], axis=0)
        for n in range(nb):
            num = num + w[:, n:n + 1] * acc_ref[n]
        o = num / den
        for h in range(n_heads):
            hs = slice(h * HEAD_DIM, (h + 1) * HEAD_DIM)
            o_ref[:, hs] = (_silu(z_ref[:, col(3, h)]) * o[h * l:(h + 1) * l]).astype(BF16)


def _moba_sample(z3, qg, kg, cache_k, cache_v, page_table, layer, slopes, col0, n_heads):
    b, l, ncols = z3.shape
    n_pages = page_table.shape[1]
    assert n_pages % PAGES_PER_BLOCK == 0, "the own block must hold no cached page"
    nb = n_pages // PAGES_PER_BLOCK
    assert nb <= HEAD_DIM
    pp = _pick(n_pages, (8, 4, 2))
    kern = functools.partial(_moba_sample_kernel, l=l, pp=pp, n_heads=n_heads, nb=nb, col0=col0,
                             past=n_pages * PAGE_SIZE, slopes=slopes)
    d = n_heads * HEAD_DIM
    gspec = pl.BlockSpec((1, HEAD_DIM), lambda i, j, pt: (0, 0))

    def page_spec(p):
        return pl.BlockSpec((None, None, n_heads, PAGE_SIZE, HEAD_DIM),
                            lambda i, j, pt: (layer, pt[i, j * pp + p], 0, 0, 0))

    new_spec = pl.BlockSpec((None, n_heads, l, HEAD_DIM), lambda i, j, pt: (i, 0, 0, 0))
    new_shape = jax.ShapeDtypeStruct((b, n_heads, l, HEAD_DIM), F32)
    assert n_heads * l <= HEAD_DIM
    stat = pltpu.VMEM((HEAD_DIM, HEAD_DIM), F32)
    return pl.pallas_call(
        kern,
        grid_spec=pltpu.PrefetchScalarGridSpec(
            num_scalar_prefetch=1,
            grid=(b, n_pages // pp),
            in_specs=[pl.BlockSpec((None, l, ncols), lambda i, j, pt: (i, 0, 0)), gspec, gspec]
                     + [page_spec(p) for p in range(pp)] * 2,
            out_specs=[pl.BlockSpec((None, l, d), lambda i, j, pt: (i, 0, 0)), new_spec, new_spec],
            scratch_shapes=[pltpu.VMEM((n_heads, l, HEAD_DIM), F32),
                            pltpu.VMEM((n_heads * HEAD_DIM, HEAD_DIM), BF16),
                            stat, stat,
                            pltpu.VMEM((nb, n_heads * l, HEAD_DIM), F32),
                            pltpu.VMEM((n_heads, HEAD_DIM, HEAD_DIM), F32)]),
        out_shape=[jax.ShapeDtypeStruct((b, l, d), BF16), new_shape, new_shape],
        compiler_params=_params("arbitrary", "arbitrary"),
        name="moba_sample",
    )(page_table, z3, qg, kg, *([cache_k] * pp), *([cache_v] * pp))


def _alibi_slopes(n):
    def geo(m):
        start = 2.0 ** (-8.0 / m)
        return [start ** (i + 1) for i in range(m)]
    if (n & (n - 1)) == 0:
        return geo(n)
    c = 2 ** int(math.floor(math.log2(n)))
    return geo(c) + geo(2 * c)[0::2][: n - c]


def kernel(x_prompt, x_sample, cache_k, cache_v, state_hgrn, state_pool, page_table, norm_gain, w_in,
           pool_w, pool_scale, hgrn_lb, hgrn_norm_gain, q_norm_gain, k_norm_gain, w_out):
    depth, d, _ = w_in.shape
    d_pool = d // 4
    d_hgrn = (d - d_pool) // 2
    d_att = d - d_pool - d_hgrn
    h_hgrn, h_att = d_hgrn // HEAD_DIM, d_att // HEAD_DIM
    col_hgrn = 2 * d_pool // HEAD_DIM
    col_att = col_hgrn + 4 * h_hgrn
    b, t, _ = x_prompt.shape
    sb, sl, _ = x_sample.shape
    past = page_table.shape[1] * PAGE_SIZE
    slopes = [float(np.float32(s)) for s in _alibi_slopes(h_att)]
    lb_cum = jnp.cumsum(jax.nn.softmax(hgrn_lb.astype(F32), axis=0), axis=0)
    lb_all = lb_cum - lb_cum[:1]
    zero_hist = jnp.zeros((b, HIST_ROWS, d_pool), F32)
    zero_state = jnp.zeros((b, h_hgrn, HEAD_DIM, HEAD_DIM), F32)

    y_p = x_prompt.reshape(b * t, d)
    y_s = x_sample.reshape(sb * sl, d)
    outs = [[] for _ in range(6)]
    kv_pages = None
    for layer in range(depth):
        w_i = w_in[layer].astype(BF16)
        w_o = w_out[layer].astype(BF16)
        w_p = pool_w[layer].astype(BF16)
        gain = norm_gain[layer][None]
        scale = pool_scale[layer][None]
        lb, ng = lb_all[layer][None], hgrn_norm_gain[layer][None]
        qg, kg = q_norm_gain[layer][None], k_norm_gain[layer][None]

        z = _inproj(y_p, gain, w_i).reshape(b, t, -1)
        oa, hist_p = _pool(z, zero_hist, w_p, scale, 0, d_pool)
        ob, s_p = _hgrn_prompt(z, lb, ng, zero_state, col_hgrn, h_hgrn)
        oc, *kv_pages = _moba_prompt(z, qg, kg, slopes, col_att, h_att, prev=kv_pages)
        y_p = _outproj(y_p, oa.reshape(b * t, -1), ob.reshape(b * t, -1), oc.reshape(b * t, -1), w_o)

        z = _inproj(y_s, gain, w_i).reshape(sb, sl, -1)
        hist16 = jnp.pad(state_pool[layer], ((0, 0), (HIST_ROWS - POOL_HIST, 0), (0, 0)))
        oa, hist_s = _pool(z, hist16, w_p, scale, past, d_pool)
        ob, s_s = _hgrn_sample(z, lb, ng, state_hgrn[layer], col_hgrn, h_hgrn)
        oc, k_s, v_s = _moba_sample(z, qg, kg, cache_k, cache_v, page_table, layer, slopes, col_att, h_att)
        y_s = _outproj(y_s, oa.reshape(sb * sl, -1), ob.reshape(sb * sl, -1), oc.reshape(sb * sl, -1), w_o)

        for lst, a in zip(outs, (s_p, hist_p[:, 1:], k_s, v_s, s_s, hist_s[:, 1:])):
            lst.append(a)
    s_p, hist_p, k_s, v_s, s_s, hist_s = (jnp.stack(o) for o in outs)
    return (y_p.reshape(b, t, d), y_s.reshape(sb, sl, d), kv_pages[0], kv_pages[1], s_p, hist_p,
            k_s, v_s, s_s, hist_s)
```

```python
import functools
import math
import types

import jax
import jax.numpy as jnp
import numpy as np
from jax import lax
from jax.experimental import pallas as pl
from jax.experimental.pallas import tpu as pltpu

F32 = jnp.float32
BF16 = jnp.bfloat16

HEAD_DIM = 128
POOL_WINDOWS = (2, 4, 8, 16)
POOL_HIST = max(POOL_WINDOWS) - 1
HIST_ROWS = POOL_HIST + 1
HGRN_CHUNK = 64
HGRN_RIDE_ROWS = 512
MOBA_BLOCK = 256
MOBA_TOPK = 3
PAGE_SIZE = 128
PAGES_PER_BLOCK = MOBA_BLOCK // PAGE_SIZE
NORM_EPS = 1e-6
MASKED = -1e30
LOG2E = math.log2(math.e)
MXU_DEPTH = 256
VMEM_LIMIT = 48 * 1024 * 1024
_NT = (((1,), (1,)), ((), ()))


def _pick(n, candidates):
    for c in candidates:
        if n % c == 0:
            return c
    raise ValueError(f"no tile for {n}")


def _params(*sem):
    return pltpu.CompilerParams(dimension_semantics=sem, vmem_limit_bytes=VMEM_LIMIT)


def _silu(x):
    return x / (1.0 + jnp.exp(-x))


def _rms_rows(x, g):
    ms = jnp.mean(x * x, axis=-1, keepdims=True)
    return x * lax.rsqrt(ms + NORM_EPS) * g


def _inproj_kernel(x_ref, g_ref, w_ref, z_ref, h_ref):
    @pl.when(pl.program_id(1) == 0)
    def _():
        h_ref[...] = _rms_rows(x_ref[...], g_ref[...]).astype(BF16)
    z_ref[...] = jnp.dot(h_ref[...], w_ref[...], preferred_element_type=F32)


def _inproj(x2d, gain, w_all, layer):
    m, d = x2d.shape
    n = w_all.shape[2]
    tm = _pick(m, (1024, 512, 256, 128, 64, 32, 16, 8))
    tn = _pick(n, (1024, 512, 256, 128))
    return pl.pallas_call(
        _inproj_kernel,
        grid=(m // tm, n // tn),
        in_specs=[pl.BlockSpec((tm, d), lambda i, j: (i, 0)),
                  pl.BlockSpec((1, d), lambda i, j: (0, 0)),
                  pl.BlockSpec((None, d, tn), lambda i, j: (layer, 0, j))],
        out_specs=pl.BlockSpec((tm, tn), lambda i, j: (i, j)),
        out_shape=jax.ShapeDtypeStruct((m, n), F32),
        scratch_shapes=[pltpu.VMEM((tm, d), BF16)],
        compiler_params=_params("arbitrary", "arbitrary"),
        name="inproj",
    )(x2d, gain, w_all)


def _outproj_kernel(x_ref, a_ref, b_ref, c_ref, w_ref, y_ref):
    mix = jnp.concatenate([a_ref[...], b_ref[...], c_ref[...]], axis=1)
    y_ref[...] = x_ref[...] + jnp.dot(mix, w_ref[...], preferred_element_type=F32)


def _outproj(x2d, oa, ob, oc, w_all, layer):
    m, d = x2d.shape
    d_mix = w_all.shape[1]
    assert oa.shape[1] + ob.shape[1] + oc.shape[1] == d_mix
    tm = _pick(m, (1024, 512, 256, 128, 64, 32, 16, 8))
    tn = _pick(d, (512, 256, 128))
    row = lambda a: pl.BlockSpec((tm, a.shape[1]), lambda i, j: (i, 0))
    return pl.pallas_call(
        _outproj_kernel,
        grid=(m // tm, d // tn),
        in_specs=[pl.BlockSpec((tm, tn), lambda i, j: (i, j)), row(oa), row(ob), row(oc),
                  pl.BlockSpec((None, d_mix, tn), lambda i, j: (layer, 0, j))],
        out_specs=pl.BlockSpec((tm, tn), lambda i, j: (i, j)),
        out_shape=jax.ShapeDtypeStruct((m, d), F32),
        compiler_params=_params("arbitrary", "arbitrary"),
        name="outproj",
    )(x2d, oa, ob, oc, w_all)


def _pool_kernel(u_ref, g_ref, hist_ref, w_ref, scale_ref, o_ref, hist_out_ref, ext_ref,
                 *, tl, pos0, group):
    t = pl.program_id(1)

    @pl.when(t == 0)
    def _():
        ext_ref[0:HIST_ROWS, :] = hist_ref[...]

    @pl.when(t > 0)
    def _():
        ext_ref[0:HIST_ROWS, :] = ext_ref[tl:tl + HIST_ROWS, :]

    ext_ref[HIST_ROWS:HIST_ROWS + tl, :] = u_ref[...]
    pos = pos0 + t * tl + lax.broadcasted_iota(jnp.int32, (tl, group), 0)
    for gi, w in enumerate(POOL_WINDOWS):
        cs = slice(gi * group, (gi + 1) * group)
        acc = ext_ref[HIST_ROWS:HIST_ROWS + tl, cs]
        for j in range(1, w):
            acc = acc + ext_ref[HIST_ROWS - j:HIST_ROWS - j + tl, cs]
        cnt = jnp.minimum(w, pos + 1).astype(F32)
        diff = acc / cnt - u_ref[:, cs]
        out = jnp.dot(diff.astype(BF16), w_ref[gi], preferred_element_type=F32) * scale_ref[:, cs]
        o_ref[:, cs] = (_silu(g_ref[:, cs]) * out).astype(BF16)
    hist_out_ref[...] = ext_ref[tl:tl + HIST_ROWS, :]


def _pool(z3, hist16, w_bf16, scale, pos0, d_pool):
    b, l, _ = z3.shape
    group = d_pool // len(POOL_WINDOWS)
    tl = _pick(l, (512, 256, 128, 64, 32, 16, 8))
    kern = functools.partial(_pool_kernel, tl=tl, pos0=pos0, group=group)
    return pl.pallas_call(
        kern,
        grid=(b, l // tl),
        in_specs=[pl.BlockSpec((None, tl, d_pool), lambda i, t: (i, t, 0)),
                  pl.BlockSpec((None, tl, d_pool), lambda i, t: (i, t, 1)),
                  pl.BlockSpec((None, HIST_ROWS, d_pool), lambda i, t: (i, 0, 0)),
                  pl.BlockSpec((len(POOL_WINDOWS), group, group), lambda i, t: (0, 0, 0)),
                  pl.BlockSpec((1, d_pool), lambda i, t: (0, 0))],
        out_specs=[pl.BlockSpec((None, tl, d_pool), lambda i, t: (i, t, 0)),
                   pl.BlockSpec((None, HIST_ROWS, d_pool), lambda i, t: (i, 0, 0))],
        out_shape=[jax.ShapeDtypeStruct((b, l, d_pool), BF16),
                   jax.ShapeDtypeStruct((b, HIST_ROWS, d_pool), F32)],
        scratch_shapes=[pltpu.VMEM((HIST_ROWS + tl, d_pool), F32)],
        compiler_params=_params("arbitrary", "arbitrary"),
        name="pool",
    )(z3, z3, hist16, w_bf16, scale)


def _cumsum_rows(x):
    n = x.shape[0]
    row = lax.broadcasted_iota(jnp.int32, x.shape, 0)
    s = 1
    while s < n:
        x = x + jnp.where(row >= s, pltpu.roll(x, s, axis=0), 0.0)
        s *= 2
    return x


def _hgrn_chunk(q_raw, z, v, lb_terms, st_ref, consts, chunk, sub, live=None):
    log_lb, log1m_lb, one_m_lb = lb_terms
    ones_kk, diag_mask, col_blk = consts
    lsig = jnp.minimum(z, 0.0) - jnp.log1p(jnp.exp(-jnp.abs(z)))
    b_ = log1m_lb + lsig
    logf = jnp.maximum(log_lb, b_) + jnp.log1p(jnp.exp(-jnp.abs(log_lb - b_)))
    k = one_m_lb * (1.0 / (1.0 + jnp.exp(z)))
    if live is not None:
        logf = jnp.where(live, logf, 0.0)
        k = jnp.where(live, k, 0.0)
    q = _silu(q_raw)
    cum = _cumsum_rows(logf)
    last = cum[chunk - 1:chunk, :]
    st = st_ref[...]
    o = lax.dot_general((q * jnp.exp(cum)).astype(BF16), st.astype(BF16),
                        (((1,), (1,)), ((), ())), preferred_element_type=F32)
    nsub = chunk // sub
    vb16 = v.astype(BF16)
    if nsub > 1:
        atts = [jnp.zeros((sub, chunk), F32)]
        for i in range(1, nsub):
            c_i = cum[i * sub - 1:i * sub, :]
            a = q[i * sub:(i + 1) * sub] * jnp.exp(cum[i * sub:(i + 1) * sub] - c_i)
            bmat = k * jnp.exp(jnp.minimum(c_i - cum, 0.0))
            att = lax.dot_general(a.astype(BF16), bmat.astype(BF16),
                                  (((1,), (1,)), ((), ())), preferred_element_type=F32)
            atts.append(jnp.where(col_blk < i, att, 0.0))
        att_all = jnp.concatenate(atts, axis=0)
        o = o + jnp.dot(att_all.astype(BF16), vb16, preferred_element_type=F32)
    diag = []
    for i in range(nsub):
        sl = slice(i * sub, (i + 1) * sub)
        cb, kb, qb, vb = cum[sl], k[sl], q[sl], v[sl]
        ws = []
        for t in range(sub):
            e = jnp.exp(jnp.minimum(cb[t:t + 1, :] - cb, 0.0))
            ws.append((qb[t:t + 1, :] * kb) * e)
        w = jnp.concatenate(ws, axis=0)
        att_rep = jnp.dot(w.astype(BF16), ones_kk, preferred_element_type=F32)
        p = jnp.where(diag_mask, att_rep * jnp.concatenate([vb] * sub, axis=0), 0.0)
        diag.append(jnp.sum(p.reshape(sub, sub, HEAD_DIM), axis=1))
    o = o + (jnp.concatenate(diag, axis=0) if nsub > 1 else diag[0])
    kd = (k * jnp.exp(last - cum)).astype(BF16)
    upd = lax.dot_general(vb16, kd, (((0,), (0,)), ((), ())), preferred_element_type=F32)
    st_ref[...] = st * jnp.exp(last) + upd
    return o


def _hgrn_consts(chunk, sub):
    ones_kk = jnp.ones((HEAD_DIM, HEAD_DIM), BF16)
    r = lax.broadcasted_iota(jnp.int32, (sub * sub, HEAD_DIM), 0)
    diag_mask = (r % sub) <= (r // sub)
    col_blk = lax.broadcasted_iota(jnp.int32, (sub, chunk), 1) // sub
    return ones_kk, diag_mask, col_blk


def _lb_terms(lb):
    return jnp.log(lb), jnp.log1p(-lb), 1.0 - lb


def _hgrn_halves(chunk):
    return [chunk >> (i + 1) for i in range(chunk.bit_length() - 1)]


def _hgrn_level_consts(chunk):
    halves = _hgrn_halves(chunk)
    nl = len(halves)
    m = np.zeros((nl + 1, chunk, chunk), np.float32)
    mask = np.zeros((nl, chunk, chunk), np.float32)
    for li, hz in enumerate(halves):
        for t in range(chunk):
            base = (t // (2 * hz)) * 2 * hz
            mid = base + hz
            if t >= mid:
                m[li, t, mid:t + 1] = 1.0
                mask[li, t, base:mid] = 1.0
            else:
                m[li, t, t + 1:mid] = 1.0
    m[nl] = np.tril(np.ones((chunk, chunk), np.float32))
    m3 = np.zeros(((nl + 1) * chunk, MXU_DEPTH), np.float32)
    m3[:, :3 * chunk] = np.tile(m.reshape((nl + 1) * chunk, chunk), (1, 3))
    return jnp.asarray(m3, BF16), jnp.asarray(mask, F32)


def _hgrn_prompt_consts(lb_ref, ng_ref, chunk):
    row = lax.broadcasted_iota(jnp.int32, (chunk, HEAD_DIM), 0)
    uppers = [(row % (2 * hz)) >= hz for hz in _hgrn_halves(chunk)]
    ones_kk = jnp.ones((HEAD_DIM, HEAD_DIM), BF16)
    kpad = jnp.zeros((MXU_DEPTH - 3 * chunk, HEAD_DIM), BF16)
    return _lb_terms(lb_ref[...]), ng_ref[...], uppers, ones_kk, kpad


def _run_phases(*phase_iters):
    live = list(phase_iters)
    while live:
        for it in list(live):
            if next(it, StopIteration) is StopIteration:
                live.remove(it)


def _hgrn_group(q_ref, f_ref, i_ref, g_ref, m3_ref, mask_ref, o_ref, st_ref, consts, rows, chunk):
    (log_lb, log1m_lb, one_m_lb), ng, uppers, ones_kk, kpad = consts
    nl = len(uppers)
    group = len(rows)
    qs, ks, vs, e2s = [], [], [], []
    for gi in range(group):
        z = f_ref[rows[gi], :]
        lsig = jnp.minimum(z, 0.0) - jnp.log1p(jnp.exp(-jnp.abs(z)))
        b_ = log1m_lb + lsig
        logf = jnp.maximum(log_lb, b_) + jnp.log1p(jnp.exp(-jnp.abs(log_lb - b_)))
        ks.append(one_m_lb * (1.0 / (1.0 + jnp.exp(z))))
        qs.append(_silu(q_ref[rows[gi], :]))
        vs.append(i_ref[rows[gi], :])
        g2 = logf * LOG2E
        hi = g2.astype(BF16)
        r1 = g2 - hi.astype(F32)
        mid = r1.astype(BF16)
        lo = (r1 - mid.astype(F32)).astype(BF16)
        e2s.append(jnp.dot(m3_ref[...], jnp.concatenate([hi, mid, lo, kpad], axis=0),
                           preferred_element_type=F32))
    yield
    atts = []
    for gi in range(group):
        att = None
        for li in range(nl):
            p = jnp.exp2(e2s[gi][li * chunk:(li + 1) * chunk])
            ab = (jnp.where(uppers[li], qs[gi], ks[gi]) * p).astype(BF16)
            a = lax.dot_general(ab, ab, _NT, preferred_element_type=F32) * mask_ref[li]
            att = a if att is None else att + a
        atts.append(att)
    yield
    outs, kds, qes, dec = [], [], [], []
    for gi in range(group):
        q, k, v = qs[gi], ks[gi], vs[gi]
        cum2 = e2s[gi][nl * chunk:(nl + 1) * chunk]
        last2 = cum2[chunk - 1:chunk, :]
        o = jnp.dot(atts[gi].astype(BF16), v.astype(BF16), preferred_element_type=F32)
        o = o + jnp.dot((q * k).astype(BF16), ones_kk, preferred_element_type=F32) * v
        outs.append(o)
        qes.append((q * jnp.exp2(cum2)).astype(BF16))
        kds.append((k * jnp.exp2(last2 - cum2)).astype(BF16))
        dec.append(jnp.exp2(last2))
    upds = [lax.dot_general(vs[gi].astype(BF16), kds[gi], (((0,), (0,)), ((), ())),
                            preferred_element_type=F32) for gi in range(group)]
    yield
    st = st_ref[...]
    for gi in range(group):
        o = outs[gi] + lax.dot_general(qes[gi], st.astype(BF16), _NT, preferred_element_type=F32)
        o_ref[rows[gi], :] = (_silu(g_ref[rows[gi], :]) * _rms_rows(o, ng)).astype(BF16)
        st = st * dec[gi] + upds[gi]
    st_ref[...] = st


def _hgrn_prompt_kernel(q_ref, f_ref, i_ref, g_ref, lb_ref, ng_ref, s0_ref, m3_ref, mask_ref,
                        o_ref, s_out_ref, st_ref, *, l, chunk, group):
    st_ref[...] = s0_ref[...].T
    consts = _hgrn_prompt_consts(lb_ref, ng_ref, chunk)

    def body(c, carry):
        rows = [pl.ds(pl.multiple_of(c * (chunk * group) + gi * chunk, chunk), chunk) for gi in range(group)]
        _run_phases(_hgrn_group(q_ref, f_ref, i_ref, g_ref, m3_ref, mask_ref, o_ref, st_ref, consts, rows, chunk))
        return carry

    lax.fori_loop(0, l // (chunk * group), body, 0)
    s_out_ref[...] = st_ref[...].T


def _hgrn_prompt(z3, lb, ng, s0, col0, n_heads):
    b, l, _ = z3.shape
    chunk = HGRN_CHUNK
    group = _pick(l // chunk, (16, 8, 4, 2, 1))
    m3, mask = _hgrn_level_consts(chunk)
    kern = functools.partial(_hgrn_prompt_kernel, l=l, chunk=chunk, group=group)
    zspec = lambda k: pl.BlockSpec((None, l, HEAD_DIM), lambda i, h: (i, 0, col0 + k * n_heads + h))
    hspec = pl.BlockSpec((1, HEAD_DIM), lambda i, h: (0, h))
    sspec = pl.BlockSpec((None, None, HEAD_DIM, HEAD_DIM), lambda i, h: (i, h, 0, 0))
    return pl.pallas_call(
        kern,
        grid=(b, n_heads),
        in_specs=[zspec(0), zspec(1), zspec(2), zspec(3), hspec, hspec, sspec,
                  pl.BlockSpec(m3.shape, lambda i, h: (0, 0)),
                  pl.BlockSpec(mask.shape, lambda i, h: (0, 0, 0))],
        out_specs=[pl.BlockSpec((None, l, HEAD_DIM), lambda i, h: (i, 0, h)), sspec],
        out_shape=[jax.ShapeDtypeStruct((b, l, n_heads * HEAD_DIM), BF16),
                   jax.ShapeDtypeStruct((b, n_heads, HEAD_DIM, HEAD_DIM), F32)],
        scratch_shapes=[pltpu.VMEM((HEAD_DIM, HEAD_DIM), F32)],
        compiler_params=_params("arbitrary", "arbitrary"),
        name="hgrn_prompt",
    )(z3, z3, z3, z3, lb, ng, s0, m3, mask)


def _hgrn_sample_kernel(z_ref, lb_ref, ng_ref, s0_ref, o_ref, s_out_ref, st_ref,
                        *, l, chunk, col0, n_heads):
    consts = _hgrn_consts(chunk, chunk)
    live = lax.broadcasted_iota(jnp.int32, (chunk, HEAD_DIM), 0) < l
    pad = jnp.zeros((chunk - l, HEAD_DIM), F32)
    for h in range(n_heads):
        col = lambda k: slice((col0 + k * n_heads + h) * HEAD_DIM, (col0 + k * n_heads + h + 1) * HEAD_DIM)
        hs = slice(h * HEAD_DIM, (h + 1) * HEAD_DIM)
        padded = lambda k: jnp.concatenate([z_ref[:, col(k)], pad], axis=0)
        st_ref[...] = s0_ref[h].T
        o = _hgrn_chunk(padded(0), padded(1), padded(2), _lb_terms(lb_ref[:, hs]), st_ref, consts,
                        chunk, chunk, live=live)
        o = _silu(z_ref[:, col(3)]) * _rms_rows(o[0:l], ng_ref[:, hs])
        o_ref[:, hs] = o.astype(BF16)
        s_out_ref[h] = st_ref[...].T


def _hgrn_sample(z3, lb, ng, s0, col0, n_heads):
    b, l, ncols = z3.shape
    chunk = 16
    kern = functools.partial(_hgrn_sample_kernel, l=l, chunk=chunk, col0=col0, n_heads=n_heads)
    d = n_heads * HEAD_DIM
    sspec = pl.BlockSpec((None, n_heads, HEAD_DIM, HEAD_DIM), lambda i: (i, 0, 0, 0))
    return pl.pallas_call(
        kern,
        grid=(b,),
        in_specs=[pl.BlockSpec((None, l, ncols), lambda i: (i, 0, 0)),
                  pl.BlockSpec((1, d), lambda i: (0, 0)),
                  pl.BlockSpec((1, d), lambda i: (0, 0)),
                  sspec],
        out_specs=[pl.BlockSpec((None, l, d), lambda i: (i, 0, 0)), sspec],
        out_shape=[jax.ShapeDtypeStruct((b, l, d), BF16),
                   jax.ShapeDtypeStruct((b, n_heads, HEAD_DIM, HEAD_DIM), F32)],
        scratch_shapes=[pltpu.VMEM((HEAD_DIM, HEAD_DIM), F32)],
        compiler_params=_params("arbitrary"),
        name="hgrn_sample",
    )(z3, lb, ng, s0)


def _topk_select(sc, n_cand):
    lane = lax.broadcasted_iota(jnp.int32, sc.shape, 1)
    rank = jnp.zeros(sc.shape, F32)
    for m in range(n_cand):
        sc_m = sc[:, m:m + 1]
        ge = jnp.where(sc_m >= sc, 1.0, 0.0)
        gt = jnp.where(sc_m > sc, 1.0, 0.0)
        rank = rank + jnp.where(lane > m, ge, gt)
    return jnp.where(rank < MOBA_TOPK, 1.0, 0.0)


def _moba_prompt_kernel(slope_ref, q_ref, k_ref, v_ref, g_ref, qg_ref, kg_ref, *rest, l, nb, n_prev):
    if n_prev:
        kprev_ref, vprev_ref = rest[:2]
        rest = rest[2:]
    o_ref, ko_ref, vo_ref, qn_ref, qa_ref, ka_ref, vb_ref, km_ref = rest
    blk, d = MOBA_BLOCK, HEAD_DIM
    nbp = -(-nb // 8) * 8
    slope2 = slope_ref[pl.program_id(1)] * LOG2E
    if n_prev:
        ko_ref[0:n_prev] = kprev_ref[...]
        vo_ref[0:n_prev] = vprev_ref[...]
    qn = _rms_rows(q_ref[...], qg_ref[...])
    qn_ref[...] = qn
    qa_ref[:, 0:d] = (qn * (d ** -0.5 * LOG2E)).astype(BF16)
    kn = _rms_rows(k_ref[...], kg_ref[...])
    ka_ref[:, 0:d] = kn.astype(BF16)
    key_blk = lax.broadcasted_iota(jnp.int32, (l, d), 0) // blk
    ka_ref[:, d:2 * d] = jnp.where(key_blk == lax.broadcasted_iota(jnp.int32, (l, d), 1), 1.0, 0.0).astype(BF16)
    ko_ref[n_prev] = kn.reshape(l // PAGE_SIZE, PAGE_SIZE, d)
    v = v_ref[...]
    vb_ref[...] = v.astype(BF16)
    vo_ref[n_prev] = v.reshape(l // PAGE_SIZE, PAGE_SIZE, d)
    km_ref[...] = jnp.zeros(km_ref.shape, F32)
    for n in range(nb):
        kblk = ko_ref[n_prev, n * PAGES_PER_BLOCK:(n + 1) * PAGES_PER_BLOCK].reshape(blk, d)
        km_ref[n:n + 1, :] = jnp.mean(kblk, axis=0, keepdims=True)
    causal = lax.broadcasted_iota(jnp.int32, (blk, blk), 1) <= lax.broadcasted_iota(jnp.int32, (blk, blk), 0)
    blk_row = lax.broadcasted_iota(jnp.int32, (nbp, blk), 0)

    def scores(i):
        rows = slice(i * blk, (i + 1) * blk)
        nk = (i + 1) * blk
        if i <= MOBA_TOPK:
            return lax.dot_general(qa_ref[rows, 0:d], ka_ref[0:nk, 0:d], _NT, preferred_element_type=F32)
        sct = lax.dot_general(km_ref[0:nbp, :], qn_ref[rows, :], _NT, preferred_element_type=F32,
                              precision=lax.Precision.HIGHEST)
        rank = jnp.zeros((nbp, blk), F32)
        for m in range(i):
            sm = sct[m:m + 1, :]
            ge = jnp.where(sm >= sct, 1.0, 0.0)
            gt = jnp.where(sm > sct, 1.0, 0.0)
            rank = rank + jnp.where(blk_row > m, ge, gt)
        drop = jnp.where((rank < MOBA_TOPK) | (blk_row >= i), 0.0, MASKED)
        drop = jnp.concatenate([drop, jnp.zeros((d - nbp, blk), F32)], axis=0).T
        qa_ref[rows, d:2 * d] = drop.astype(BF16)
        return lax.dot_general(qa_ref[rows, :], ka_ref[0:nk, :], _NT, preferred_element_type=F32)

    s_next = scores(0)
    for i in range(nb):
        rows = slice(i * blk, (i + 1) * blk)
        n_past = i * blk
        s = s_next
        if i + 1 < nb:
            s_next = scores(i + 1)
        kpos = lax.broadcasted_iota(jnp.int32, (1, n_past + blk), 1) - n_past
        s = s + slope2 * kpos.astype(F32)
        s_own = jnp.where(causal, s[:, n_past:], MASKED)
        m = jnp.max(s_own, axis=-1, keepdims=True)
        if i:
            s_past = s[:, :n_past]
            m = jnp.maximum(m, jnp.max(s_past, axis=-1, keepdims=True))
        p_own = jnp.exp2(s_own - m)
        den = jnp.sum(p_own, axis=-1, keepdims=True)
        o = jnp.dot(p_own.astype(BF16), vb_ref[n_past:n_past + blk, :], preferred_element_type=F32)
        if i:
            p_past = jnp.exp2(s_past - m)
            den = den + jnp.sum(p_past, axis=-1, keepdims=True)
            o = o + jnp.dot(p_past.astype(BF16), vb_ref[0:n_past, :], preferred_element_type=F32)
        o_ref[rows, :] = (_silu(g_ref[rows, :]) * (o / den)).astype(BF16)


def _moba_prompt(z3, qg, kg, slopes, col0, n_heads, prev=None):
    b, l, _ = z3.shape
    assert l % MOBA_BLOCK == 0
    nb = l // MOBA_BLOCK
    assert nb <= HEAD_DIM
    npg = l // PAGE_SIZE
    n_prev = 0 if prev is None else prev[0].shape[0]
    kern = functools.partial(_moba_prompt_kernel, l=l, nb=nb, n_prev=n_prev)
    zspec = lambda k: pl.BlockSpec((None, l, HEAD_DIM), lambda i, h: (i, 0, col0 + k * n_heads + h))
    gspec = pl.BlockSpec((1, HEAD_DIM), lambda i, h: (0, 0))
    pspec = lambda n: pl.BlockSpec((n, None, npg, None, PAGE_SIZE, HEAD_DIM), lambda i, h: (0, i, 0, h, 0, 0))
    page_shape = jax.ShapeDtypeStruct((n_prev + 1, b, npg, n_heads, PAGE_SIZE, HEAD_DIM), F32)
    prev_specs = [pspec(n_prev)] * 2 if n_prev else []
    return pl.pallas_call(
        kern,
        grid=(b, n_heads),
        in_specs=[pl.BlockSpec(memory_space=pltpu.SMEM),
                  zspec(0), zspec(1), zspec(2), zspec(3), gspec, gspec] + prev_specs,
        out_specs=[pl.BlockSpec((None, l, HEAD_DIM), lambda i, h: (i, 0, h)), pspec(n_prev + 1), pspec(n_prev + 1)],
        out_shape=[jax.ShapeDtypeStruct((b, l, n_heads * HEAD_DIM), BF16), page_shape, page_shape],
        scratch_shapes=[pltpu.VMEM((l, HEAD_DIM), F32),
                        pltpu.VMEM((l, 2 * HEAD_DIM), BF16),
                        pltpu.VMEM((l, 2 * HEAD_DIM), BF16),
                        pltpu.VMEM((l, HEAD_DIM), BF16),
                        pltpu.VMEM((HEAD_DIM, HEAD_DIM), F32)],
        compiler_params=_params("arbitrary", "arbitrary"),
        name="moba_prompt",
    )(jnp.asarray(slopes, F32), z3, z3, z3, z3, qg, kg, *(prev or ()))


def _sample_parts(R, *, l, pp, n_heads, nb, col0, past, slopes):
    z_ref, qg_ref, kg_ref, k_refs, v_refs = R.z_ref, R.qg_ref, R.kg_ref, R.k_refs, R.v_refs
    o_ref, ko_ref, vo_ref = R.o_ref, R.ko_ref, R.vo_ref
    qn_ref, wq_ref, m_ref, l_ref, acc_ref, km_ref = R.qn_ref, R.wq_ref, R.m_ref, R.l_ref, R.acc_ref, R.km_ref
    blk = MOBA_BLOCK
    qscale = HEAD_DIM ** -0.5 * LOG2E
    col = lambda k, h: slice((col0 + k * n_heads + h) * HEAD_DIM, (col0 + k * n_heads + h + 1) * HEAD_DIM)
    zpad = jnp.zeros((HEAD_DIM - l, HEAD_DIM), F32)

    def init():
        for h in range(n_heads):
            qn = _rms_rows(z_ref[:, col(0, h)], qg_ref[...])
            qn_ref[h] = qn
            qt = jnp.concatenate([qn * qscale, zpad], axis=0).T
            if h:
                qt = pltpu.roll(qt, h * l, axis=1)
            wq_ref[h * HEAD_DIM:(h + 1) * HEAD_DIM, :] = qt.astype(BF16)
            ko_ref[h] = _rms_rows(z_ref[:, col(1, h)], kg_ref[...])
            vo_ref[h] = z_ref[:, col(2, h)]
        km_ref[...] = jnp.zeros(km_ref.shape, F32)
        m_ref[...] = jnp.zeros(m_ref.shape, F32)
        l_ref[...] = jnp.zeros(l_ref.shape, F32)

    def step(j):
        lane = lax.broadcasted_iota(jnp.int32, (1, HEAD_DIM), 1)
        slope2 = jnp.zeros((1, HEAD_DIM), F32)
        for h in range(n_heads):
            slope2 = jnp.where(lane // l == h, slopes[h] * LOG2E, slope2)
        lq = (lane % l).astype(F32)
        key_bias = slope2 * lax.broadcasted_iota(jnp.int32, (blk, HEAD_DIM), 0).astype(F32)
        bps = pp // PAGES_PER_BLOCK
        pages = lambda bi: range(bi * PAGES_PER_BLOCK, (bi + 1) * PAGES_PER_BLOCK)
        scores = []
        for bi in range(bps):
            kcat = jnp.concatenate(
                [jnp.concatenate([k_refs[p][h].astype(BF16) for h in range(n_heads)], axis=1) for p in pages(bi)],
                axis=0)
            scores.append(jnp.dot(kcat, wq_ref[...], preferred_element_type=F32))
        yield
        probs = []
        for bi in range(bps):
            n = j * bps + bi
            s = scores[bi] + key_bias + slope2 * ((n * blk - past).astype(F32) - lq)
            m = jnp.max(s, axis=0, keepdims=True)
            p_ = jnp.exp2(s - m)
            m_ref[pl.ds(n, 1), :] = m
            l_ref[pl.ds(n, 1), :] = jnp.sum(p_, axis=0, keepdims=True)
            probs.append(p_.T)
            for h in range(n_heads):
                kblk = jnp.concatenate([k_refs[p][h] for p in pages(bi)], axis=0)
                km_ref[h, pl.ds(n, 1), :] = jnp.mean(kblk, axis=0, keepdims=True)
        yield
        for bi in range(bps):
            n = j * bps + bi
            for h in range(n_heads):
                vblk = jnp.concatenate([v_refs[p][h].astype(BF16) for p in pages(bi)], axis=0)
                acc_ref[n, h * l:(h + 1) * l, :] = jnp.dot(probs[bi][h * l:(h + 1) * l, :].astype(BF16), vblk,
                                                           preferred_element_type=F32)

    def final():
        hl = n_heads * l
        ro = lax.broadcasted_iota(jnp.int32, (hl, HEAD_DIM), 0)
        co = lax.broadcasted_iota(jnp.int32, (hl, HEAD_DIM), 1)
        qi = ro % l
        slope_rows = jnp.zeros((hl, HEAD_DIM), F32)
        for h in range(n_heads):
            slope_rows = jnp.where(ro // l == h, slopes[h] * LOG2E, slope_rows)
        sc, s, vown = [], [], []
        for h in range(n_heads):
            qn = qn_ref[h]
            sc.append(lax.dot_general(qn, km_ref[h], _NT, preferred_element_type=F32,
                                      precision=lax.Precision.HIGHEST))
            kown = jnp.concatenate([ko_ref[h], zpad], axis=0).astype(BF16)
            vown.append(jnp.concatenate([vo_ref[h], zpad], axis=0).astype(BF16))
            s.append(lax.dot_general((qn * qscale).astype(BF16), kown, _NT, preferred_element_type=F32))
        keep = (_topk_select(jnp.concatenate(sc, axis=0), nb) > 0.5) & (co < nb)
        mh = m_ref[...].T[0:hl, :]
        lh = l_ref[...].T[0:hl, :]
        s = jnp.concatenate(s, axis=0)
        s = jnp.where(co <= qi, s - slope_rows * (qi - co).astype(F32), MASKED)
        mx = jnp.maximum(jnp.max(s, axis=-1, keepdims=True),
                         jnp.max(jnp.where(keep, mh, MASKED), axis=-1, keepdims=True))
        w = jnp.where(keep, jnp.exp2(mh - mx), 0.0)
        p_ = jnp.exp2(s - mx)
        den = jnp.sum(p_, axis=-1, keepdims=True) + jnp.sum(w * lh, axis=-1, keepdims=True)
        num = jnp.concatenate(
            [jnp.dot(p_[h * l:(h + 1) * l].astype(BF16), vown[h], preferred_element_type=F32)
             for h in range(n_heads)], axis=0)
        for n in range(nb):
            num = num + w[:, n:n + 1] * acc_ref[n]
        o = num / den
        for h in range(n_heads):
            hs = slice(h * HEAD_DIM, (h + 1) * HEAD_DIM)
            o_ref[:, hs] = (_silu(z_ref[:, col(3, h)]) * o[h * l:(h + 1) * l]).astype(BF16)

    return init, step, final


def _unpack_sample_refs(z_ref, qg_ref, kg_ref, pages, outs, scratch, pp):
    return types.SimpleNamespace(
        z_ref=z_ref, qg_ref=qg_ref, kg_ref=kg_ref, k_refs=pages[:pp], v_refs=pages[pp:],
        o_ref=outs[0], ko_ref=outs[1], vo_ref=outs[2],
        qn_ref=scratch[0], wq_ref=scratch[1], m_ref=scratch[2], l_ref=scratch[3], acc_ref=scratch[4],
        km_ref=scratch[5])


def _moba_sample_kernel(pt_ref, z_ref, qg_ref, kg_ref, *rest, pp, **static):
    del pt_ref
    refs = _unpack_sample_refs(z_ref, qg_ref, kg_ref, rest[:2 * pp], rest[2 * pp:2 * pp + 3], rest[2 * pp + 3:], pp)
    init, step, final = _sample_parts(refs, pp=pp, **static)
    j = pl.program_id(1)
    pl.when(j == 0)(init)
    _run_phases(step(j))
    pl.when(j == pl.num_programs(1) - 1)(final)


def _sample_hgrn_kernel(pt_ref, z_ref, qg_ref, kg_ref, *rest, pp, chunk, hsteps, n_hsteps, **static):
    del pt_ref
    hq_ref, hf_ref, hi_ref, hg_ref, lb_ref, ng_ref, s0_ref, m3_ref, mask_ref = rest[2 * pp:2 * pp + 9]
    outs = rest[2 * pp + 9:2 * pp + 14]
    scratch = rest[2 * pp + 14:]
    ho_ref, hs_ref, st_ref = outs[3], outs[4], scratch[6]
    refs = _unpack_sample_refs(z_ref, qg_ref, kg_ref, rest[:2 * pp], outs[:3], scratch[:6], pp)
    init, step, final = _sample_parts(refs, pp=pp, **static)
    j = pl.program_id(1)
    s = pl.program_id(0) * pl.num_programs(1) + j
    active = s < n_hsteps
    t = s % hsteps
    rows = [pl.ds(gi * chunk, chunk) for gi in range(ho_ref.shape[0] // chunk)]
    pl.when(j == 0)(init)

    @pl.when(active & (t == 0))
    def _():
        st_ref[...] = s0_ref[...].T

    @pl.when(active)
    def _():
        consts = _hgrn_prompt_consts(lb_ref, ng_ref, chunk)
        _run_phases(_hgrn_group(hq_ref, hf_ref, hi_ref, hg_ref, m3_ref, mask_ref, ho_ref, st_ref, consts, rows, chunk),
                    step(j))

    @pl.when(jnp.logical_not(active))
    def _():
        _run_phases(step(j))

    @pl.when(active & (t == hsteps - 1))
    def _():
        hs_ref[...] = st_ref[...].T

    pl.when(j == pl.num_programs(1) - 1)(final)


def _sample_page_group(n_pages):
    return _pick(n_pages, (8, 4, 2))


def _can_ride(b, l, n_heads, sb, n_pages):
    if l % HGRN_RIDE_ROWS:
        return False
    return b * n_heads * (l // HGRN_RIDE_ROWS) <= sb * (n_pages // _sample_page_group(n_pages))


def _moba_sample(z3, qg, kg, cache_k, cache_v, page_table, layer, slopes, col0, n_heads, ride=None):
    b, l, ncols = z3.shape
    n_pages = page_table.shape[1]
    assert n_pages % PAGES_PER_BLOCK == 0, "the own block must hold no cached page"
    nb = n_pages // PAGES_PER_BLOCK
    assert nb <= HEAD_DIM
    assert n_heads * l <= HEAD_DIM
    pp = _sample_page_group(n_pages)
    nj = n_pages // pp
    static = dict(l=l, pp=pp, n_heads=n_heads, nb=nb, col0=col0, past=n_pages * PAGE_SIZE, slopes=slopes)
    d = n_heads * HEAD_DIM
    gspec = pl.BlockSpec((1, HEAD_DIM), lambda i, j, pt: (0, 0))

    def page_spec(p):
        return pl.BlockSpec((None, None, n_heads, PAGE_SIZE, HEAD_DIM),
                            lambda i, j, pt: (layer, pt[i, j * pp + p], 0, 0, 0))

    new_spec = pl.BlockSpec((None, n_heads, l, HEAD_DIM), lambda i, j, pt: (i, 0, 0, 0))
    new_shape = jax.ShapeDtypeStruct((b, n_heads, l, HEAD_DIM), F32)
    stat = pltpu.VMEM((HEAD_DIM, HEAD_DIM), F32)
    in_specs = ([pl.BlockSpec((None, l, ncols), lambda i, j, pt: (i, 0, 0)), gspec, gspec]
                + [page_spec(p) for p in range(pp)] * 2)
    out_specs = [pl.BlockSpec((None, l, d), lambda i, j, pt: (i, 0, 0)), new_spec, new_spec]
    out_shape = [jax.ShapeDtypeStruct((b, l, d), BF16), new_shape, new_shape]
    scratch = [pltpu.VMEM((n_heads, l, HEAD_DIM), F32),
               pltpu.VMEM((n_heads * HEAD_DIM, HEAD_DIM), BF16),
               stat, stat,
               pltpu.VMEM((nb, n_heads * l, HEAD_DIM), F32),
               pltpu.VMEM((n_heads, HEAD_DIM, HEAD_DIM), F32)]
    args = [page_table, z3, qg, kg, *([cache_k] * pp), *([cache_v] * pp)]
    if ride is None:
        kern = functools.partial(_moba_sample_kernel, **static)
        name = "moba_sample"
    else:
        zp, lb, ng, s0, hcol0, hh = ride
        pb, pl_rows, _ = zp.shape
        assert _can_ride(pb, pl_rows, hh, b, n_pages)
        tl = HGRN_RIDE_ROWS
        hsteps = pl_rows // tl
        n_hsteps = pb * hh * hsteps
        m3, mask = _hgrn_level_consts(HGRN_CHUNK)

        def where(i, j):
            s = jnp.minimum(i * nj + j, n_hsteps - 1)
            return (s // hsteps) // hh, (s // hsteps) % hh, s % hsteps

        def zspec(k):
            def index(i, j, pt):
                bp, h, t = where(i, j)
                return bp, t, hcol0 + k * hh + h
            return pl.BlockSpec((None, tl, HEAD_DIM), index)

        hspec = pl.BlockSpec((1, HEAD_DIM), lambda i, j, pt: (0, where(i, j)[1]))
        sspec = pl.BlockSpec((None, None, HEAD_DIM, HEAD_DIM),
                             lambda i, j, pt: (where(i, j)[0], where(i, j)[1], 0, 0))
        ospec = pl.BlockSpec((None, tl, HEAD_DIM),
                             lambda i, j, pt: (where(i, j)[0], where(i, j)[2], where(i, j)[1]))
        in_specs += [zspec(0), zspec(1), zspec(2), zspec(3), hspec, hspec, sspec,
                     pl.BlockSpec(m3.shape, lambda i, j, pt: (0, 0)),
                     pl.BlockSpec(mask.shape, lambda i, j, pt: (0, 0, 0))]
        out_specs += [ospec, sspec]
        out_shape += [jax.ShapeDtypeStruct((pb, pl_rows, hh * HEAD_DIM), BF16),
                      jax.ShapeDtypeStruct((pb, hh, HEAD_DIM, HEAD_DIM), F32)]
        scratch += [pltpu.VMEM((HEAD_DIM, HEAD_DIM), F32)]
        args += [zp, zp, zp, zp, lb, ng, s0, m3, mask]
        kern = functools.partial(_sample_hgrn_kernel, chunk=HGRN_CHUNK, hsteps=hsteps, n_hsteps=n_hsteps, **static)
        name = "moba_sample_hgrn_prompt"
    return pl.pallas_call(
        kern,
        grid_spec=pltpu.PrefetchScalarGridSpec(
            num_scalar_prefetch=1, grid=(b, nj), in_specs=in_specs, out_specs=out_specs,
            scratch_shapes=scratch),
        out_shape=out_shape,
        compiler_params=_params("arbitrary", "arbitrary"),
        name=name,
    )(*args)


def _alibi_slopes(n):
    def geo(m):
        start = 2.0 ** (-8.0 / m)
        return [start ** (i + 1) for i in range(m)]
    if (n & (n - 1)) == 0:
        return geo(n)
    c = 2 ** int(math.floor(math.log2(n)))
    return geo(c) + geo(2 * c)[0::2][: n - c]


def kernel(x_prompt, x_sample, cache_k, cache_v, state_hgrn, state_pool, page_table, norm_gain, w_in,
           pool_w, pool_scale, hgrn_lb, hgrn_norm_gain, q_norm_gain, k_norm_gain, w_out):
    depth, d, _ = w_in.shape
    d_pool = d // 4
    d_hgrn = (d - d_pool) // 2
    d_att = d - d_pool - d_hgrn
    h_hgrn, h_att = d_hgrn // HEAD_DIM, d_att // HEAD_DIM
    col_hgrn = 2 * d_pool // HEAD_DIM
    col_att = col_hgrn + 4 * h_hgrn
    b, t, _ = x_prompt.shape
    sb, sl, _ = x_sample.shape
    past = page_table.shape[1] * PAGE_SIZE
    slopes = [float(np.float32(s)) for s in _alibi_slopes(h_att)]
    lb_cum = jnp.cumsum(jax.nn.softmax(hgrn_lb.astype(F32), axis=0), axis=0)
    lb_all = lb_cum - lb_cum[:1]
    zero_hist = jnp.zeros((b, HIST_ROWS, d_pool), F32)
    zero_state = jnp.zeros((b, h_hgrn, HEAD_DIM, HEAD_DIM), F32)

    y_p = x_prompt.reshape(b * t, d)
    y_s = x_sample.reshape(sb * sl, d)
    outs = [[] for _ in range(6)]
    kv_pages = None
    w_i = w_in.astype(BF16)
    w_o = w_out.astype(BF16)
    for layer in range(depth):
        w_p = pool_w[layer].astype(BF16)
        gain = norm_gain[layer][None]
        scale = pool_scale[layer][None]
        lb, ng = lb_all[layer][None], hgrn_norm_gain[layer][None]
        qg, kg = q_norm_gain[layer][None], k_norm_gain[layer][None]

        z = _inproj(y_p, gain, w_i, layer).reshape(b, t, -1)
        zs = _inproj(y_s, gain, w_i, layer).reshape(sb, sl, -1)
        sample_args = (zs, qg, kg, cache_k, cache_v, page_table, layer, slopes, col_att, h_att)
        if _can_ride(b, t, h_hgrn, sb, page_table.shape[1]):
            oc_s, k_s, v_s, ob, s_p = _moba_sample(*sample_args, ride=(z, lb, ng, zero_state, col_hgrn, h_hgrn))
        else:
            oc_s, k_s, v_s = _moba_sample(*sample_args)
            ob, s_p = _hgrn_prompt(z, lb, ng, zero_state, col_hgrn, h_hgrn)
        oa, hist_p = _pool(z, zero_hist, w_p, scale, 0, d_pool)
        oc, *kv_pages = _moba_prompt(z, qg, kg, slopes, col_att, h_att, prev=kv_pages)
        y_p = _outproj(y_p, oa.reshape(b * t, -1), ob.reshape(b * t, -1), oc.reshape(b * t, -1), w_o, layer)

        hist16 = jnp.pad(state_pool[layer], ((0, 0), (HIST_ROWS - POOL_HIST, 0), (0, 0)))
        oa, hist_s = _pool(zs, hist16, w_p, scale, past, d_pool)
        ob, s_s = _hgrn_sample(zs, lb, ng, state_hgrn[layer], col_hgrn, h_hgrn)
        y_s = _outproj(y_s, oa.reshape(sb * sl, -1), ob.reshape(sb * sl, -1), oc_s.reshape(sb * sl, -1), w_o, layer)

        for lst, a in zip(outs, (s_p, hist_p[:, 1:], k_s, v_s, s_s, hist_s[:, 1:])):
            lst.append(a)
    s_p, hist_p, k_s, v_s, s_s, hist_s = (jnp.stack(o) for o in outs)
    return (y_p.reshape(b, t, d), y_s.reshape(sb, sl, d), kv_pages[0], kv_pages[1], s_p, hist_p,
            k_s, v_s, s_s, hist_s)
```

```python
import functools
import math
import types

import jax
import jax.numpy as jnp
import numpy as np
from jax import lax
from jax.experimental import pallas as pl
from jax.experimental.pallas import tpu as pltpu

F32 = jnp.float32
BF16 = jnp.bfloat16

HEAD_DIM = 128
POOL_WINDOWS = (2, 4, 8, 16)
POOL_HIST = max(POOL_WINDOWS) - 1
HIST_ROWS = POOL_HIST + 1
HGRN_CHUNK = 64
HGRN_RIDE_ROWS = 512
MOBA_BLOCK = 256
MOBA_TOPK = 3
PAGE_SIZE = 128
PAGES_PER_BLOCK = MOBA_BLOCK // PAGE_SIZE
NORM_EPS = 1e-6
MASKED = -1e30
LOG2E = math.log2(math.e)
MXU_DEPTH = 256
VMEM_LIMIT = 48 * 1024 * 1024
_NT = (((1,), (1,)), ((), ()))


def _pick(n, candidates):
    for c in candidates:
        if n % c == 0:
            return c
    raise ValueError(f"no tile for {n}")


def _params(*sem):
    return pltpu.CompilerParams(dimension_semantics=sem, vmem_limit_bytes=VMEM_LIMIT)


def _silu(x):
    return x / (1.0 + jnp.exp(-x))


def _rms_rows(x, g):
    ms = jnp.mean(x * x, axis=-1, keepdims=True)
    return x * lax.rsqrt(ms + NORM_EPS) * g


def _inproj_kernel(x_ref, g_ref, w_ref, z_ref, h_ref):
    @pl.when(pl.program_id(1) == 0)
    def _():
        h_ref[...] = _rms_rows(x_ref[...], g_ref[...]).astype(BF16)
    z_ref[...] = jnp.dot(h_ref[...], w_ref[...], preferred_element_type=F32)


def _inproj(x2d, gain, w_all, layer):
    m, d = x2d.shape
    n = w_all.shape[2]
    tm = _pick(m, (1024, 512, 256, 128, 64, 32, 16, 8))
    tn = _pick(n, (1024, 512, 256, 128))
    return pl.pallas_call(
        _inproj_kernel,
        grid=(m // tm, n // tn),
        in_specs=[pl.BlockSpec((tm, d), lambda i, j: (i, 0)),
                  pl.BlockSpec((1, d), lambda i, j: (0, 0)),
                  pl.BlockSpec((None, d, tn), lambda i, j: (layer, 0, j))],
        out_specs=pl.BlockSpec((tm, tn), lambda i, j: (i, j)),
        out_shape=jax.ShapeDtypeStruct((m, n), F32),
        scratch_shapes=[pltpu.VMEM((tm, d), BF16)],
        compiler_params=_params("arbitrary", "arbitrary"),
        name="inproj",
    )(x2d, gain, w_all)


def _outproj_kernel(x_ref, a_ref, b_ref, c_ref, w_ref, y_ref):
    acc, row0 = None, 0
    for o_ref in (a_ref, b_ref, c_ref):
        part = jnp.dot(o_ref[...], w_ref[row0:row0 + o_ref.shape[1], :], preferred_element_type=F32)
        acc = part if acc is None else acc + part
        row0 += o_ref.shape[1]
    y_ref[...] = x_ref[...] + acc


def _outproj(x2d, oa, ob, oc, w_all, layer):
    m, d = x2d.shape
    d_mix = w_all.shape[1]
    assert oa.shape[1] + ob.shape[1] + oc.shape[1] == d_mix
    wide = 2 * d_mix * d * 2 <= VMEM_LIMIT // 2
    tn = d if wide else _pick(d, (512, 256, 128))
    tm = _pick(m, (512, 256, 128, 64, 32, 16, 8) if wide else (1024, 512, 256, 128, 64, 32, 16, 8))
    row = lambda a: pl.BlockSpec((tm, a.shape[1]), lambda i, j: (i, 0))
    return pl.pallas_call(
        _outproj_kernel,
        grid=(m // tm, d // tn),
        in_specs=[pl.BlockSpec((tm, tn), lambda i, j: (i, j)), row(oa), row(ob), row(oc),
                  pl.BlockSpec((None, d_mix, tn), lambda i, j: (layer, 0, j))],
        out_specs=pl.BlockSpec((tm, tn), lambda i, j: (i, j)),
        out_shape=jax.ShapeDtypeStruct((m, d), F32),
        compiler_params=_params("arbitrary", "arbitrary"),
        name="outproj",
    )(x2d, oa, ob, oc, w_all)


def _pool_tile(u_ref, g_ref, w_ref, scale_ref, o_ref, hist_out_ref, ext_ref, pos_first, tl, group):
    ext_ref[HIST_ROWS:HIST_ROWS + tl, :] = u_ref[...]
    pos = pos_first + lax.broadcasted_iota(jnp.int32, (tl, group), 0)
    for gi, w in enumerate(POOL_WINDOWS):
        cs = slice(gi * group, (gi + 1) * group)
        acc = ext_ref[HIST_ROWS:HIST_ROWS + tl, cs]
        for j in range(1, w):
            acc = acc + ext_ref[HIST_ROWS - j:HIST_ROWS - j + tl, cs]
        cnt = jnp.minimum(w, pos + 1).astype(F32)
        diff = acc / cnt - u_ref[:, cs]
        out = jnp.dot(diff.astype(BF16), w_ref[gi], preferred_element_type=F32) * scale_ref[:, cs]
        o_ref[:, cs] = (_silu(g_ref[:, cs]) * out).astype(BF16)
    hist_out_ref[...] = ext_ref[tl:tl + HIST_ROWS, :]


def _pool_kernel(u_ref, g_ref, hist_ref, w_ref, scale_ref, o_ref, hist_out_ref, ext_ref,
                 *, tl, pos0, group):
    t = pl.program_id(1)

    @pl.when(t == 0)
    def _():
        ext_ref[0:HIST_ROWS, :] = hist_ref[...]

    @pl.when(t > 0)
    def _():
        ext_ref[0:HIST_ROWS, :] = ext_ref[tl:tl + HIST_ROWS, :]

    _pool_tile(u_ref, g_ref, w_ref, scale_ref, o_ref, hist_out_ref, ext_ref, pos0 + t * tl, tl, group)


def _pool(z3, hist16, w_bf16, scale, pos0, d_pool):
    b, l, _ = z3.shape
    group = d_pool // len(POOL_WINDOWS)
    tl = _pick(l, (512, 256, 128, 64, 32, 16, 8))
    kern = functools.partial(_pool_kernel, tl=tl, pos0=pos0, group=group)
    return pl.pallas_call(
        kern,
        grid=(b, l // tl),
        in_specs=[pl.BlockSpec((None, tl, d_pool), lambda i, t: (i, t, 0)),
                  pl.BlockSpec((None, tl, d_pool), lambda i, t: (i, t, 1)),
                  pl.BlockSpec((None, HIST_ROWS, d_pool), lambda i, t: (i, 0, 0)),
                  pl.BlockSpec((len(POOL_WINDOWS), group, group), lambda i, t: (0, 0, 0)),
                  pl.BlockSpec((1, d_pool), lambda i, t: (0, 0))],
        out_specs=[pl.BlockSpec((None, tl, d_pool), lambda i, t: (i, t, 0)),
                   pl.BlockSpec((None, HIST_ROWS, d_pool), lambda i, t: (i, 0, 0))],
        out_shape=[jax.ShapeDtypeStruct((b, l, d_pool), BF16),
                   jax.ShapeDtypeStruct((b, HIST_ROWS, d_pool), F32)],
        scratch_shapes=[pltpu.VMEM((HIST_ROWS + tl, d_pool), F32)],
        compiler_params=_params("arbitrary", "arbitrary"),
        name="pool",
    )(z3, z3, hist16, w_bf16, scale)


def _cumsum_rows(x):
    n = x.shape[0]
    row = lax.broadcasted_iota(jnp.int32, x.shape, 0)
    s = 1
    while s < n:
        x = x + jnp.where(row >= s, pltpu.roll(x, s, axis=0), 0.0)
        s *= 2
    return x


def _hgrn_chunk(q_raw, z, v, lb_terms, st_ref, consts, chunk, sub, live=None):
    log_lb, log1m_lb, one_m_lb = lb_terms
    ones_kk, diag_mask, col_blk = consts
    lsig = jnp.minimum(z, 0.0) - jnp.log1p(jnp.exp(-jnp.abs(z)))
    b_ = log1m_lb + lsig
    logf = jnp.maximum(log_lb, b_) + jnp.log1p(jnp.exp(-jnp.abs(log_lb - b_)))
    k = one_m_lb * (1.0 / (1.0 + jnp.exp(z)))
    if live is not None:
        logf = jnp.where(live, logf, 0.0)
        k = jnp.where(live, k, 0.0)
    q = _silu(q_raw)
    cum = _cumsum_rows(logf)
    last = cum[chunk - 1:chunk, :]
    st = st_ref[...]
    o = lax.dot_general((q * jnp.exp(cum)).astype(BF16), st.astype(BF16),
                        (((1,), (1,)), ((), ())), preferred_element_type=F32)
    nsub = chunk // sub
    vb16 = v.astype(BF16)
    if nsub > 1:
        atts = [jnp.zeros((sub, chunk), F32)]
        for i in range(1, nsub):
            c_i = cum[i * sub - 1:i * sub, :]
            a = q[i * sub:(i + 1) * sub] * jnp.exp(cum[i * sub:(i + 1) * sub] - c_i)
            bmat = k * jnp.exp(jnp.minimum(c_i - cum, 0.0))
            att = lax.dot_general(a.astype(BF16), bmat.astype(BF16),
                                  (((1,), (1,)), ((), ())), preferred_element_type=F32)
            atts.append(jnp.where(col_blk < i, att, 0.0))
        att_all = jnp.concatenate(atts, axis=0)
        o = o + jnp.dot(att_all.astype(BF16), vb16, preferred_element_type=F32)
    diag = []
    for i in range(nsub):
        sl = slice(i * sub, (i + 1) * sub)
        cb, kb, qb, vb = cum[sl], k[sl], q[sl], v[sl]
        ws = []
        for t in range(sub):
            e = jnp.exp(jnp.minimum(cb[t:t + 1, :] - cb, 0.0))
            ws.append((qb[t:t + 1, :] * kb) * e)
        w = jnp.concatenate(ws, axis=0)
        att_rep = jnp.dot(w.astype(BF16), ones_kk, preferred_element_type=F32)
        p = jnp.where(diag_mask, att_rep * jnp.concatenate([vb] * sub, axis=0), 0.0)
        diag.append(jnp.sum(p.reshape(sub, sub, HEAD_DIM), axis=1))
    o = o + (jnp.concatenate(diag, axis=0) if nsub > 1 else diag[0])
    kd = (k * jnp.exp(last - cum)).astype(BF16)
    upd = lax.dot_general(vb16, kd, (((0,), (0,)), ((), ())), preferred_element_type=F32)
    st_ref[...] = st * jnp.exp(last) + upd
    return o


def _hgrn_consts(chunk, sub):
    ones_kk = jnp.ones((HEAD_DIM, HEAD_DIM), BF16)
    r = lax.broadcasted_iota(jnp.int32, (sub * sub, HEAD_DIM), 0)
    diag_mask = (r % sub) <= (r // sub)
    col_blk = lax.broadcasted_iota(jnp.int32, (sub, chunk), 1) // sub
    return ones_kk, diag_mask, col_blk


def _lb_terms(lb):
    return jnp.log(lb), jnp.log1p(-lb), 1.0 - lb


def _hgrn_halves(chunk):
    return [chunk >> (i + 1) for i in range(chunk.bit_length() - 1)]


def _hgrn_level_consts(chunk):
    halves = _hgrn_halves(chunk)
    nl = len(halves)
    m = np.zeros((nl + 1, chunk, chunk), np.float32)
    mask = np.zeros((nl, chunk, chunk), np.float32)
    for li, hz in enumerate(halves):
        for t in range(chunk):
            base = (t // (2 * hz)) * 2 * hz
            mid = base + hz
            if t >= mid:
                m[li, t, mid:t + 1] = 1.0
                mask[li, t, base:mid] = 1.0
            else:
                m[li, t, t + 1:mid] = 1.0
    m[nl] = np.tril(np.ones((chunk, chunk), np.float32))
    m3 = np.zeros(((nl + 1) * chunk, MXU_DEPTH), np.float32)
    m3[:, :3 * chunk] = np.tile(m.reshape((nl + 1) * chunk, chunk), (1, 3))
    return jnp.asarray(m3, BF16), jnp.asarray(mask, F32)


def _hgrn_prompt_consts(lb_ref, ng_ref, chunk):
    row = lax.broadcasted_iota(jnp.int32, (chunk, HEAD_DIM), 0)
    uppers = [(row % (2 * hz)) >= hz for hz in _hgrn_halves(chunk)]
    ones_kk = jnp.ones((HEAD_DIM, HEAD_DIM), BF16)
    kpad = jnp.zeros((MXU_DEPTH - 3 * chunk, HEAD_DIM), BF16)
    return _lb_terms(lb_ref[...]), ng_ref[...], uppers, ones_kk, kpad


def _run_phases(*phase_iters):
    live = list(phase_iters)
    while live:
        for it in list(live):
            if next(it, StopIteration) is StopIteration:
                live.remove(it)


def _hgrn_group(q_ref, f_ref, i_ref, g_ref, m3_ref, mask_ref, o_ref, st_ref, consts, rows, chunk):
    (log_lb, log1m_lb, one_m_lb), ng, uppers, ones_kk, kpad = consts
    nl = len(uppers)
    group = len(rows)
    qs, ks, vs, e2s = [], [], [], []
    for gi in range(group):
        z = f_ref[rows[gi], :]
        lsig = jnp.minimum(z, 0.0) - jnp.log1p(jnp.exp(-jnp.abs(z)))
        b_ = log1m_lb + lsig
        logf = jnp.maximum(log_lb, b_) + jnp.log1p(jnp.exp(-jnp.abs(log_lb - b_)))
        ks.append(one_m_lb * (1.0 / (1.0 + jnp.exp(z))))
        qs.append(_silu(q_ref[rows[gi], :]))
        vs.append(i_ref[rows[gi], :])
        g2 = logf * LOG2E
        hi = g2.astype(BF16)
        r1 = g2 - hi.astype(F32)
        mid = r1.astype(BF16)
        lo = (r1 - mid.astype(F32)).astype(BF16)
        e2s.append(jnp.dot(m3_ref[...], jnp.concatenate([hi, mid, lo, kpad], axis=0),
                           preferred_element_type=F32))
    yield
    atts = []
    for gi in range(group):
        att = None
        for li in range(nl):
            p = jnp.exp2(e2s[gi][li * chunk:(li + 1) * chunk])
            ab = (jnp.where(uppers[li], qs[gi], ks[gi]) * p).astype(BF16)
            a = lax.dot_general(ab, ab, _NT, preferred_element_type=F32) * mask_ref[li]
            att = a if att is None else att + a
        atts.append(att)
    yield
    outs, kds, qes, dec = [], [], [], []
    for gi in range(group):
        q, k, v = qs[gi], ks[gi], vs[gi]
        cum2 = e2s[gi][nl * chunk:(nl + 1) * chunk]
        last2 = cum2[chunk - 1:chunk, :]
        o = jnp.dot(atts[gi].astype(BF16), v.astype(BF16), preferred_element_type=F32)
        o = o + jnp.dot((q * k).astype(BF16), ones_kk, preferred_element_type=F32) * v
        outs.append(o)
        qes.append((q * jnp.exp2(cum2)).astype(BF16))
        kds.append((k * jnp.exp2(last2 - cum2)).astype(BF16))
        dec.append(jnp.exp2(last2))
    upds = [lax.dot_general(vs[gi].astype(BF16), kds[gi], (((0,), (0,)), ((), ())),
                            preferred_element_type=F32) for gi in range(group)]
    yield
    st = st_ref[...]
    for gi in range(group):
        o = outs[gi] + lax.dot_general(qes[gi], st.astype(BF16), _NT, preferred_element_type=F32)
        o_ref[rows[gi], :] = (_silu(g_ref[rows[gi], :]) * _rms_rows(o, ng)).astype(BF16)
        st = st * dec[gi] + upds[gi]
    st_ref[...] = st


def _hgrn_prompt_kernel(q_ref, f_ref, i_ref, g_ref, lb_ref, ng_ref, s0_ref, m3_ref, mask_ref,
                        o_ref, s_out_ref, st_ref, *, l, chunk, group):
    st_ref[...] = s0_ref[...].T
    consts = _hgrn_prompt_consts(lb_ref, ng_ref, chunk)

    def body(c, carry):
        rows = [pl.ds(pl.multiple_of(c * (chunk * group) + gi * chunk, chunk), chunk) for gi in range(group)]
        _run_phases(_hgrn_group(q_ref, f_ref, i_ref, g_ref, m3_ref, mask_ref, o_ref, st_ref, consts, rows, chunk))
        return carry

    lax.fori_loop(0, l // (chunk * group), body, 0)
    s_out_ref[...] = st_ref[...].T


def _hgrn_prompt(z3, lb, ng, s0, col0, n_heads):
    b, l, _ = z3.shape
    chunk = HGRN_CHUNK
    group = _pick(l // chunk, (16, 8, 4, 2, 1))
    m3, mask = _hgrn_level_consts(chunk)
    kern = functools.partial(_hgrn_prompt_kernel, l=l, chunk=chunk, group=group)
    zspec = lambda k: pl.BlockSpec((None, l, HEAD_DIM), lambda i, h: (i, 0, col0 + k * n_heads + h))
    hspec = pl.BlockSpec((1, HEAD_DIM), lambda i, h: (0, h))
    sspec = pl.BlockSpec((None, None, HEAD_DIM, HEAD_DIM), lambda i, h: (i, h, 0, 0))
    return pl.pallas_call(
        kern,
        grid=(b, n_heads),
        in_specs=[zspec(0), zspec(1), zspec(2), zspec(3), hspec, hspec, sspec,
                  pl.BlockSpec(m3.shape, lambda i, h: (0, 0)),
                  pl.BlockSpec(mask.shape, lambda i, h: (0, 0, 0))],
        out_specs=[pl.BlockSpec((None, l, HEAD_DIM), lambda i, h: (i, 0, h)), sspec],
        out_shape=[jax.ShapeDtypeStruct((b, l, n_heads * HEAD_DIM), BF16),
                   jax.ShapeDtypeStruct((b, n_heads, HEAD_DIM, HEAD_DIM), F32)],
        scratch_shapes=[pltpu.VMEM((HEAD_DIM, HEAD_DIM), F32)],
        compiler_params=_params("arbitrary", "arbitrary"),
        name="hgrn_prompt",
    )(z3, z3, z3, z3, lb, ng, s0, m3, mask)


def _hgrn_sample_kernel(z_ref, lb_ref, ng_ref, s0_ref, o_ref, s_out_ref, st_ref,
                        *, l, chunk, col0, n_heads):
    consts = _hgrn_consts(chunk, chunk)
    live = lax.broadcasted_iota(jnp.int32, (chunk, HEAD_DIM), 0) < l
    pad = jnp.zeros((chunk - l, HEAD_DIM), F32)
    col = lambda k, h: slice((col0 + k * n_heads + h) * HEAD_DIM, (col0 + k * n_heads + h + 1) * HEAD_DIM)
    head = lambda h: slice(h * HEAD_DIM, (h + 1) * HEAD_DIM)
    padded = lambda k, h: jnp.concatenate([z_ref[:, col(k, h)], pad], axis=0)
    for h in range(n_heads):
        st_ref[...] = s0_ref[h].T
        o = _hgrn_chunk(padded(0, h), padded(1, h), padded(2, h), _lb_terms(lb_ref[:, head(h)]), st_ref, consts,
                        chunk, chunk, live=live)
        o = _silu(z_ref[:, col(3, h)]) * _rms_rows(o[0:l], ng_ref[:, head(h)])
        o_ref[:, head(h)] = o.astype(BF16)
        s_out_ref[h] = st_ref[...].T


def _hgrn_sample(z3, lb, ng, s0_all, layer, col0, n_heads):
    b, l, ncols = z3.shape
    chunk = 16
    kern = functools.partial(_hgrn_sample_kernel, l=l, chunk=chunk, col0=col0, n_heads=n_heads)
    d = n_heads * HEAD_DIM
    sspec = pl.BlockSpec((None, n_heads, HEAD_DIM, HEAD_DIM), lambda i: (i, 0, 0, 0))
    return pl.pallas_call(
        kern,
        grid=(b,),
        in_specs=[pl.BlockSpec((None, l, ncols), lambda i: (i, 0, 0)),
                  pl.BlockSpec((1, d), lambda i: (0, 0)),
                  pl.BlockSpec((1, d), lambda i: (0, 0)),
                  pl.BlockSpec((None, None, n_heads, HEAD_DIM, HEAD_DIM), lambda i: (layer, i, 0, 0, 0))],
        out_specs=[pl.BlockSpec((None, l, d), lambda i: (i, 0, 0)), sspec],
        out_shape=[jax.ShapeDtypeStruct((b, l, d), BF16),
                   jax.ShapeDtypeStruct((b, n_heads, HEAD_DIM, HEAD_DIM), F32)],
        scratch_shapes=[pltpu.VMEM((HEAD_DIM, HEAD_DIM), F32)],
        compiler_params=_params("arbitrary"),
        name="hgrn_sample",
    )(z3, lb, ng, s0_all)


def _topk_select(sc, n_cand):
    lane = lax.broadcasted_iota(jnp.int32, sc.shape, 1)
    rank = jnp.zeros(sc.shape, F32)
    for m in range(n_cand):
        sc_m = sc[:, m:m + 1]
        ge = jnp.where(sc_m >= sc, 1.0, 0.0)
        gt = jnp.where(sc_m > sc, 1.0, 0.0)
        rank = rank + jnp.where(lane > m, ge, gt)
    return jnp.where(rank < MOBA_TOPK, 1.0, 0.0)


def _moba_prompt_kernel(slope_ref, q_ref, k_ref, v_ref, g_ref, qg_ref, kg_ref, *rest, l, nb, n_prev):
    if n_prev:
        kprev_ref, vprev_ref = rest[:2]
        rest = rest[2:]
    o_ref, ko_ref, vo_ref, qn_ref, qa_ref, ka_ref, vb_ref, km_ref = rest
    blk, d = MOBA_BLOCK, HEAD_DIM
    nbp = -(-nb // 8) * 8
    slope2 = slope_ref[pl.program_id(1)] * LOG2E
    if n_prev:
        ko_ref[0:n_prev] = kprev_ref[...]
        vo_ref[0:n_prev] = vprev_ref[...]
    qn = _rms_rows(q_ref[...], qg_ref[...])
    qn_ref[...] = qn
    qa_ref[:, 0:d] = (qn * (d ** -0.5 * LOG2E)).astype(BF16)
    kn = _rms_rows(k_ref[...], kg_ref[...])
    ka_ref[:, 0:d] = kn.astype(BF16)
    key_blk = lax.broadcasted_iota(jnp.int32, (l, d), 0) // blk
    ka_ref[:, d:2 * d] = jnp.where(key_blk == lax.broadcasted_iota(jnp.int32, (l, d), 1), 1.0, 0.0).astype(BF16)
    ko_ref[n_prev] = kn.reshape(l // PAGE_SIZE, PAGE_SIZE, d)
    v = v_ref[...]
    vb_ref[...] = v.astype(BF16)
    vo_ref[n_prev] = v.reshape(l // PAGE_SIZE, PAGE_SIZE, d)
    km_ref[...] = jnp.zeros(km_ref.shape, F32)
    for n in range(nb):
        kblk = ko_ref[n_prev, n * PAGES_PER_BLOCK:(n + 1) * PAGES_PER_BLOCK].reshape(blk, d)
        km_ref[n:n + 1, :] = jnp.mean(kblk, axis=0, keepdims=True)
    causal = lax.broadcasted_iota(jnp.int32, (blk, blk), 1) <= lax.broadcasted_iota(jnp.int32, (blk, blk), 0)
    blk_row = lax.broadcasted_iota(jnp.int32, (nbp, blk), 0)

    def scores(i):
        rows = slice(i * blk, (i + 1) * blk)
        nk = (i + 1) * blk
        if i <= MOBA_TOPK:
            return lax.dot_general(qa_ref[rows, 0:d], ka_ref[0:nk, 0:d], _NT, preferred_element_type=F32)
        sct = lax.dot_general(km_ref[0:nbp, :], qn_ref[rows, :], _NT, preferred_element_type=F32,
                              precision=lax.Precision.HIGHEST)
        rank = jnp.zeros((nbp, blk), F32)
        for m in range(i):
            sm = sct[m:m + 1, :]
            ge = jnp.where(sm >= sct, 1.0, 0.0)
            gt = jnp.where(sm > sct, 1.0, 0.0)
            rank = rank + jnp.where(blk_row > m, ge, gt)
        drop = jnp.where((rank < MOBA_TOPK) | (blk_row >= i), 0.0, MASKED)
        drop = jnp.concatenate([drop, jnp.zeros((d - nbp, blk), F32)], axis=0).T
        qa_ref[rows, d:2 * d] = drop.astype(BF16)
        return lax.dot_general(qa_ref[rows, :], ka_ref[0:nk, :], _NT, preferred_element_type=F32)

    s_next = scores(0)
    for i in range(nb):
        rows = slice(i * blk, (i + 1) * blk)
        n_past = i * blk
        s = s_next
        if i + 1 < nb:
            s_next = scores(i + 1)
        kpos = lax.broadcasted_iota(jnp.int32, (1, n_past + blk), 1) - n_past
        s = s + slope2 * kpos.astype(F32)
        s_own = jnp.where(causal, s[:, n_past:], MASKED)
        m = jnp.max(s_own, axis=-1, keepdims=True)
        if i:
            s_past = s[:, :n_past]
            m = jnp.maximum(m, jnp.max(s_past, axis=-1, keepdims=True))
        p_own = jnp.exp2(s_own - m)
        den = jnp.sum(p_own, axis=-1, keepdims=True)
        o = jnp.dot(p_own.astype(BF16), vb_ref[n_past:n_past + blk, :], preferred_element_type=F32)
        if i:
            p_past = jnp.exp2(s_past - m)
            den = den + jnp.sum(p_past, axis=-1, keepdims=True)
            o = o + jnp.dot(p_past.astype(BF16), vb_ref[0:n_past, :], preferred_element_type=F32)
        o_ref[rows, :] = (_silu(g_ref[rows, :]) * (o / den)).astype(BF16)


def _moba_prompt(z3, qg, kg, slopes, col0, n_heads, prev=None):
    b, l, _ = z3.shape
    assert l % MOBA_BLOCK == 0
    nb = l // MOBA_BLOCK
    assert nb <= HEAD_DIM
    npg = l // PAGE_SIZE
    n_prev = 0 if prev is None else prev[0].shape[0]
    kern = functools.partial(_moba_prompt_kernel, l=l, nb=nb, n_prev=n_prev)
    zspec = lambda k: pl.BlockSpec((None, l, HEAD_DIM), lambda i, h: (i, 0, col0 + k * n_heads + h))
    gspec = pl.BlockSpec((1, HEAD_DIM), lambda i, h: (0, 0))
    pspec = lambda n: pl.BlockSpec((n, None, npg, None, PAGE_SIZE, HEAD_DIM), lambda i, h: (0, i, 0, h, 0, 0))
    page_shape = jax.ShapeDtypeStruct((n_prev + 1, b, npg, n_heads, PAGE_SIZE, HEAD_DIM), F32)
    prev_specs = [pspec(n_prev)] * 2 if n_prev else []
    return pl.pallas_call(
        kern,
        grid=(b, n_heads),
        in_specs=[pl.BlockSpec(memory_space=pltpu.SMEM),
                  zspec(0), zspec(1), zspec(2), zspec(3), gspec, gspec] + prev_specs,
        out_specs=[pl.BlockSpec((None, l, HEAD_DIM), lambda i, h: (i, 0, h)), pspec(n_prev + 1), pspec(n_prev + 1)],
        out_shape=[jax.ShapeDtypeStruct((b, l, n_heads * HEAD_DIM), BF16), page_shape, page_shape],
        scratch_shapes=[pltpu.VMEM((l, HEAD_DIM), F32),
                        pltpu.VMEM((l, 2 * HEAD_DIM), BF16),
                        pltpu.VMEM((l, 2 * HEAD_DIM), BF16),
                        pltpu.VMEM((l, HEAD_DIM), BF16),
                        pltpu.VMEM((HEAD_DIM, HEAD_DIM), F32)],
        compiler_params=_params("arbitrary", "arbitrary"),
        name="moba_prompt",
    )(jnp.asarray(slopes, F32), z3, z3, z3, z3, qg, kg, *(prev or ()))


def _sample_parts(R, *, l, pp, n_heads, nb, col0, past, slopes):
    z_ref, qg_ref, kg_ref, k_refs, v_refs = R.z_ref, R.qg_ref, R.kg_ref, R.k_refs, R.v_refs
    o_ref, ko_ref, vo_ref = R.o_ref, R.ko_ref, R.vo_ref
    qn_ref, wq_ref, m_ref, l_ref, acc_ref, km_ref = R.qn_ref, R.wq_ref, R.m_ref, R.l_ref, R.acc_ref, R.km_ref
    blk = MOBA_BLOCK
    qscale = HEAD_DIM ** -0.5 * LOG2E
    col = lambda k, h: slice((col0 + k * n_heads + h) * HEAD_DIM, (col0 + k * n_heads + h + 1) * HEAD_DIM)
    zpad = jnp.zeros((HEAD_DIM - l, HEAD_DIM), F32)

    def init():
        for h in range(n_heads):
            qn = _rms_rows(z_ref[:, col(0, h)], qg_ref[...])
            qn_ref[h] = qn
            qt = jnp.concatenate([qn * qscale, zpad], axis=0).T
            if h:
                qt = pltpu.roll(qt, h * l, axis=1)
            wq_ref[h * HEAD_DIM:(h + 1) * HEAD_DIM, :] = qt.astype(BF16)
            ko_ref[h] = _rms_rows(z_ref[:, col(1, h)], kg_ref[...])
            vo_ref[h] = z_ref[:, col(2, h)]
        km_ref[...] = jnp.zeros(km_ref.shape, F32)
        m_ref[...] = jnp.zeros(m_ref.shape, F32)
        l_ref[...] = jnp.zeros(l_ref.shape, F32)

    def step(j):
        lane = lax.broadcasted_iota(jnp.int32, (1, HEAD_DIM), 1)
        slope2 = jnp.zeros((1, HEAD_DIM), F32)
        for h in range(n_heads):
            slope2 = jnp.where(lane // l == h, slopes[h] * LOG2E, slope2)
        lq = (lane % l).astype(F32)
        key_bias = slope2 * lax.broadcasted_iota(jnp.int32, (blk, HEAD_DIM), 0).astype(F32)
        bps = pp // PAGES_PER_BLOCK
        pages = lambda bi: range(bi * PAGES_PER_BLOCK, (bi + 1) * PAGES_PER_BLOCK)
        scores = []
        for bi in range(bps):
            kcat = jnp.concatenate(
                [jnp.concatenate([k_refs[p][h].astype(BF16) for h in range(n_heads)], axis=1) for p in pages(bi)],
                axis=0)
            scores.append(jnp.dot(kcat, wq_ref[...], preferred_element_type=F32))
        yield
        probs = []
        for bi in range(bps):
            n = j * bps + bi
            s = scores[bi] + key_bias + slope2 * ((n * blk - past).astype(F32) - lq)
            m = jnp.max(s, axis=0, keepdims=True)
            p_ = jnp.exp2(s - m)
            m_ref[pl.ds(n, 1), :] = m
            l_ref[pl.ds(n, 1), :] = jnp.sum(p_, axis=0, keepdims=True)
            probs.append(p_.T)
            for h in range(n_heads):
                kblk = jnp.concatenate([k_refs[p][h] for p in pages(bi)], axis=0)
                km_ref[h, pl.ds(n, 1), :] = jnp.mean(kblk, axis=0, keepdims=True)
        yield
        for bi in range(bps):
            n = j * bps + bi
            for h in range(n_heads):
                vblk = jnp.concatenate([v_refs[p][h].astype(BF16) for p in pages(bi)], axis=0)
                acc_ref[n, h * l:(h + 1) * l, :] = jnp.dot(probs[bi][h * l:(h + 1) * l, :].astype(BF16), vblk,
                                                           preferred_element_type=F32)

    def final():
        hl = n_heads * l
        ro = lax.broadcasted_iota(jnp.int32, (hl, HEAD_DIM), 0)
        co = lax.broadcasted_iota(jnp.int32, (hl, HEAD_DIM), 1)
        qi = ro % l
        slope_rows = jnp.zeros((hl, HEAD_DIM), F32)
        for h in range(n_heads):
            slope_rows = jnp.where(ro // l == h, slopes[h] * LOG2E, slope_rows)
        sc, s, vown = [], [], []
        for h in range(n_heads):
            qn = qn_ref[h]
            sc.append(lax.dot_general(qn, km_ref[h], _NT, preferred_element_type=F32,
                                      precision=lax.Precision.HIGHEST))
            kown = jnp.concatenate([ko_ref[h], zpad], axis=0).astype(BF16)
            vown.append(jnp.concatenate([vo_ref[h], zpad], axis=0).astype(BF16))
            s.append(lax.dot_general((qn * qscale).astype(BF16), kown, _NT, preferred_element_type=F32))
        keep = (_topk_select(jnp.concatenate(sc, axis=0), nb) > 0.5) & (co < nb)
        mh = m_ref[...].T[0:hl, :]
        lh = l_ref[...].T[0:hl, :]
        s = jnp.concatenate(s, axis=0)
        s = jnp.where(co <= qi, s - slope_rows * (qi - co).astype(F32), MASKED)
        mx = jnp.maximum(jnp.max(s, axis=-1, keepdims=True),
                         jnp.max(jnp.where(keep, mh, MASKED), axis=-1, keepdims=True))
        w = jnp.where(keep, jnp.exp2(mh - mx), 0.0)
        p_ = jnp.exp2(s - mx)
        den = jnp.sum(p_, axis=-1, keepdims=True) + jnp.sum(w * lh, axis=-1, keepdims=True)
        num = jnp.concatenate(
            [jnp.dot(p_[h * l:(h + 1) * l].astype(BF16), vown[h], preferred_element_type=F32)
             for h in range(n_heads)], axis=0)
        for n in range(nb):
            num = num + w[:, n:n + 1] * acc_ref[n]
        o = num / den
        for h in range(n_heads):
            hs = slice(h * HEAD_DIM, (h + 1) * HEAD_DIM)
            o_ref[:, hs] = (_silu(z_ref[:, col(3, h)]) * o[h * l:(h + 1) * l]).astype(BF16)

    return init, step, final


def _unpack_sample_refs(z_ref, qg_ref, kg_ref, pages, outs, scratch, pp):
    return types.SimpleNamespace(
        z_ref=z_ref, qg_ref=qg_ref, kg_ref=kg_ref, k_refs=pages[:pp], v_refs=pages[pp:],
        o_ref=outs[0], ko_ref=outs[1], vo_ref=outs[2],
        qn_ref=scratch[0], wq_ref=scratch[1], m_ref=scratch[2], l_ref=scratch[3], acc_ref=scratch[4],
        km_ref=scratch[5])


def _moba_sample_kernel(pt_ref, z_ref, qg_ref, kg_ref, *rest, pp, **static):
    del pt_ref
    refs = _unpack_sample_refs(z_ref, qg_ref, kg_ref, rest[:2 * pp], rest[2 * pp:2 * pp + 3], rest[2 * pp + 3:], pp)
    init, step, final = _sample_parts(refs, pp=pp, **static)
    j = pl.program_id(1)
    pl.when(j == 0)(init)
    _run_phases(step(j))
    pl.when(j == pl.num_programs(1) - 1)(final)


def _sample_hgrn_kernel(pt_ref, z_ref, qg_ref, kg_ref, *rest, pp, chunk, hsteps, n_hsteps, n_psteps, **static):
    del pt_ref
    hq_ref, hf_ref, hi_ref, hg_ref, lb_ref, ng_ref, s0_ref, m3_ref, mask_ref = rest[2 * pp:2 * pp + 9]
    pu_ref, pg_ref, pw_ref, pscale_ref = rest[2 * pp + 9:2 * pp + 13]
    outs = rest[2 * pp + 13:2 * pp + 20]
    scratch = rest[2 * pp + 20:]
    ho_ref, hs_ref, po_ref, phist_ref = outs[3:7]
    st_ref, ext_ref = scratch[6:8]
    refs = _unpack_sample_refs(z_ref, qg_ref, kg_ref, rest[:2 * pp], outs[:3], scratch[:6], pp)
    init, step, final = _sample_parts(refs, pp=pp, **static)
    j = pl.program_id(1)
    s = pl.program_id(0) * pl.num_programs(1) + j
    in_hgrn = s < n_hsteps
    in_pool = jnp.logical_and(s >= n_hsteps, s < n_hsteps + n_psteps)
    t = s % hsteps
    tp = (s - n_hsteps) % hsteps
    tl = ho_ref.shape[0]
    rows = [pl.ds(gi * chunk, chunk) for gi in range(tl // chunk)]
    pl.when(j == 0)(init)

    @pl.when(in_hgrn & (t == 0))
    def _():
        st_ref[...] = s0_ref[...].T

    @pl.when(in_pool & (tp == 0))
    def _():
        ext_ref[0:HIST_ROWS, :] = jnp.zeros((HIST_ROWS, ext_ref.shape[1]), F32)

    @pl.when(in_pool & (tp > 0))
    def _():
        ext_ref[0:HIST_ROWS, :] = ext_ref[tl:tl + HIST_ROWS, :]

    @pl.when(in_hgrn)
    def _():
        consts = _hgrn_prompt_consts(lb_ref, ng_ref, chunk)
        _run_phases(_hgrn_group(hq_ref, hf_ref, hi_ref, hg_ref, m3_ref, mask_ref, ho_ref, st_ref, consts, rows, chunk),
                    step(j))

    @pl.when(in_pool)
    def _():
        _run_phases(step(j))
        _pool_tile(pu_ref, pg_ref, pw_ref, pscale_ref, po_ref, phist_ref, ext_ref, tp * tl, tl,
                   pu_ref.shape[1] // len(POOL_WINDOWS))

    @pl.when(s >= n_hsteps + n_psteps)
    def _():
        _run_phases(step(j))

    @pl.when(in_hgrn & (t == hsteps - 1))
    def _():
        hs_ref[...] = st_ref[...].T

    pl.when(j == pl.num_programs(1) - 1)(final)


def _sample_page_group(n_pages):
    return _pick(n_pages, (8, 4, 2))


def _can_ride(b, l, n_heads, sb, n_pages):
    if l % HGRN_RIDE_ROWS:
        return False
    return b * (n_heads + 1) * (l // HGRN_RIDE_ROWS) <= sb * (n_pages // _sample_page_group(n_pages))


def _moba_sample(z3, qg, kg, cache_k, cache_v, page_table, layer, slopes, col0, n_heads, ride=None):
    b, l, ncols = z3.shape
    n_pages = page_table.shape[1]
    assert n_pages % PAGES_PER_BLOCK == 0, "the own block must hold no cached page"
    nb = n_pages // PAGES_PER_BLOCK
    assert nb <= HEAD_DIM
    assert n_heads * l <= HEAD_DIM
    pp = _sample_page_group(n_pages)
    nj = n_pages // pp
    static = dict(l=l, pp=pp, n_heads=n_heads, nb=nb, col0=col0, past=n_pages * PAGE_SIZE, slopes=slopes)
    d = n_heads * HEAD_DIM
    gspec = pl.BlockSpec((1, HEAD_DIM), lambda i, j, pt: (0, 0))

    def page_spec(p):
        return pl.BlockSpec((None, None, n_heads, PAGE_SIZE, HEAD_DIM),
                            lambda i, j, pt: (layer, pt[i, j * pp + p], 0, 0, 0))

    new_spec = pl.BlockSpec((None, n_heads, l, HEAD_DIM), lambda i, j, pt: (i, 0, 0, 0))
    new_shape = jax.ShapeDtypeStruct((b, n_heads, l, HEAD_DIM), F32)
    stat = pltpu.VMEM((HEAD_DIM, HEAD_DIM), F32)
    in_specs = ([pl.BlockSpec((None, l, ncols), lambda i, j, pt: (i, 0, 0)), gspec, gspec]
                + [page_spec(p) for p in range(pp)] * 2)
    out_specs = [pl.BlockSpec((None, l, d), lambda i, j, pt: (i, 0, 0)), new_spec, new_spec]
    out_shape = [jax.ShapeDtypeStruct((b, l, d), BF16), new_shape, new_shape]
    scratch = [pltpu.VMEM((n_heads, l, HEAD_DIM), F32),
               pltpu.VMEM((n_heads * HEAD_DIM, HEAD_DIM), BF16),
               stat, stat,
               pltpu.VMEM((nb, n_heads * l, HEAD_DIM), F32),
               pltpu.VMEM((n_heads, HEAD_DIM, HEAD_DIM), F32)]
    args = [page_table, z3, qg, kg, *([cache_k] * pp), *([cache_v] * pp)]
    if ride is None:
        kern = functools.partial(_moba_sample_kernel, **static)
        name = "moba_sample"
    else:
        zp, lb, ng, s0, hcol0, hh, w_pool, pool_scale, d_pool = ride
        pb, pl_rows, _ = zp.shape
        assert _can_ride(pb, pl_rows, hh, b, n_pages)
        tl = HGRN_RIDE_ROWS
        hsteps = pl_rows // tl
        n_hsteps = pb * hh * hsteps
        n_psteps = pb * hsteps
        m3, mask = _hgrn_level_consts(HGRN_CHUNK)

        def pool_tile(i, j):
            q = jnp.clip(i * nj + j - n_hsteps, 0, n_psteps - 1)
            return q // hsteps, q % hsteps

        def where(i, j):
            s = jnp.minimum(i * nj + j, n_hsteps - 1)
            return (s // hsteps) // hh, (s // hsteps) % hh, s % hsteps

        def zspec(k):
            def index(i, j, pt):
                bp, h, t = where(i, j)
                return bp, t, hcol0 + k * hh + h
            return pl.BlockSpec((None, tl, HEAD_DIM), index)

        hspec = pl.BlockSpec((1, HEAD_DIM), lambda i, j, pt: (0, where(i, j)[1]))
        sspec = pl.BlockSpec((None, None, HEAD_DIM, HEAD_DIM),
                             lambda i, j, pt: (where(i, j)[0], where(i, j)[1], 0, 0))
        ospec = pl.BlockSpec((None, tl, HEAD_DIM),
                             lambda i, j, pt: (where(i, j)[0], where(i, j)[2], where(i, j)[1]))
        pool_in = lambda c: pl.BlockSpec((None, tl, d_pool), lambda i, j, pt: (*pool_tile(i, j), c))
        hist_spec = pl.BlockSpec((None, HIST_ROWS, d_pool), lambda i, j, pt: (pool_tile(i, j)[0], 0, 0))
        in_specs += [zspec(0), zspec(1), zspec(2), zspec(3), hspec, hspec, sspec,
                     pl.BlockSpec(m3.shape, lambda i, j, pt: (0, 0)),
                     pl.BlockSpec(mask.shape, lambda i, j, pt: (0, 0, 0)),
                     pool_in(0), pool_in(1),
                     pl.BlockSpec(w_pool.shape, lambda i, j, pt: (0, 0, 0)),
                     pl.BlockSpec((1, d_pool), lambda i, j, pt: (0, 0))]
        out_specs += [ospec, sspec, pool_in(0), hist_spec]
        out_shape += [jax.ShapeDtypeStruct((pb, pl_rows, hh * HEAD_DIM), BF16),
                      jax.ShapeDtypeStruct((pb, hh, HEAD_DIM, HEAD_DIM), F32),
                      jax.ShapeDtypeStruct((pb, pl_rows, d_pool), BF16),
                      jax.ShapeDtypeStruct((pb, HIST_ROWS, d_pool), F32)]
        scratch += [pltpu.VMEM((HEAD_DIM, HEAD_DIM), F32), pltpu.VMEM((HIST_ROWS + tl, d_pool), F32)]
        args += [zp, zp, zp, zp, lb, ng, s0, m3, mask, zp, zp, w_pool, pool_scale]
        kern = functools.partial(_sample_hgrn_kernel, chunk=HGRN_CHUNK, hsteps=hsteps, n_hsteps=n_hsteps,
                                 n_psteps=n_psteps, **static)
        name = "moba_sample_prompt_riders"
    return pl.pallas_call(
        kern,
        grid_spec=pltpu.PrefetchScalarGridSpec(
            num_scalar_prefetch=1, grid=(b, nj), in_specs=in_specs, out_specs=out_specs,
            scratch_shapes=scratch),
        out_shape=out_shape,
        compiler_params=_params("arbitrary", "arbitrary"),
        name=name,
    )(*args)


def _alibi_slopes(n):
    def geo(m):
        start = 2.0 ** (-8.0 / m)
        return [start ** (i + 1) for i in range(m)]
    if (n & (n - 1)) == 0:
        return geo(n)
    c = 2 ** int(math.floor(math.log2(n)))
    return geo(c) + geo(2 * c)[0::2][: n - c]


def kernel(x_prompt, x_sample, cache_k, cache_v, state_hgrn, state_pool, page_table, norm_gain, w_in,
           pool_w, pool_scale, hgrn_lb, hgrn_norm_gain, q_norm_gain, k_norm_gain, w_out):
    depth, d, _ = w_in.shape
    d_pool = d // 4
    d_hgrn = (d - d_pool) // 2
    d_att = d - d_pool - d_hgrn
    h_hgrn, h_att = d_hgrn // HEAD_DIM, d_att // HEAD_DIM
    col_hgrn = 2 * d_pool // HEAD_DIM
    col_att = col_hgrn + 4 * h_hgrn
    b, t, _ = x_prompt.shape
    sb, sl, _ = x_sample.shape
    past = page_table.shape[1] * PAGE_SIZE
    slopes = [float(np.float32(s)) for s in _alibi_slopes(h_att)]
    lb_cum = jnp.cumsum(jax.nn.softmax(hgrn_lb.astype(F32), axis=0), axis=0)
    lb_all = lb_cum - lb_cum[:1]
    zero_hist = jnp.zeros((b, HIST_ROWS, d_pool), F32)
    zero_state = jnp.zeros((b, h_hgrn, HEAD_DIM, HEAD_DIM), F32)

    y_p = x_prompt.reshape(b * t, d)
    y_s = x_sample.reshape(sb * sl, d)
    outs = [[] for _ in range(6)]
    kv_pages = None
    w_i = w_in.astype(BF16)
    w_o = w_out.astype(BF16)
    for layer in range(depth):
        w_p = pool_w[layer].astype(BF16)
        gain = norm_gain[layer][None]
        scale = pool_scale[layer][None]
        lb, ng = lb_all[layer][None], hgrn_norm_gain[layer][None]
        qg, kg = q_norm_gain[layer][None], k_norm_gain[layer][None]

        z = _inproj(y_p, gain, w_i, layer).reshape(b, t, -1)
        zs = _inproj(y_s, gain, w_i, layer).reshape(sb, sl, -1)
        sample_args = (zs, qg, kg, cache_k, cache_v, page_table, layer, slopes, col_att, h_att)
        if _can_ride(b, t, h_hgrn, sb, page_table.shape[1]):
            oc_s, k_s, v_s, ob, s_p, oa, hist_p = _moba_sample(
                *sample_args, ride=(z, lb, ng, zero_state, col_hgrn, h_hgrn, w_p, scale, d_pool))
        else:
            oc_s, k_s, v_s = _moba_sample(*sample_args)
            ob, s_p = _hgrn_prompt(z, lb, ng, zero_state, col_hgrn, h_hgrn)
            oa, hist_p = _pool(z, zero_hist, w_p, scale, 0, d_pool)
        oc, *kv_pages = _moba_prompt(z, qg, kg, slopes, col_att, h_att, prev=kv_pages)
        y_p = _outproj(y_p, oa.reshape(b * t, -1), ob.reshape(b * t, -1), oc.reshape(b * t, -1), w_o, layer)

        hist16 = jnp.pad(state_pool[layer], ((0, 0), (HIST_ROWS - POOL_HIST, 0), (0, 0)))
        oa, hist_s = _pool(zs, hist16, w_p, scale, past, d_pool)
        ob, s_s = _hgrn_sample(zs, lb, ng, state_hgrn, layer, col_hgrn, h_hgrn)
        y_s = _outproj(y_s, oa.reshape(sb * sl, -1), ob.reshape(sb * sl, -1), oc_s.reshape(sb * sl, -1), w_o, layer)

        for lst, a in zip(outs, (s_p, hist_p[:, 1:], k_s, v_s, s_s, hist_s[:, 1:])):
            lst.append(a)
    s_p, hist_p, k_s, v_s, s_s, hist_s = (jnp.stack(o) for o in outs)
    return (y_p.reshape(b, t, d), y_s.reshape(sb, sl, d), kv_pages[0], kv_pages[1], s_p, hist_p,
            k_s, v_s, s_s, hist_s)
```

```python
import functools
import math
import types

import jax
import jax.numpy as jnp
import numpy as np
from jax import lax
from jax.experimental import pallas as pl
from jax.experimental.pallas import tpu as pltpu

F32 = jnp.float32
BF16 = jnp.bfloat16

HEAD_DIM = 128
POOL_WINDOWS = (2, 4, 8, 16)
POOL_HIST = max(POOL_WINDOWS) - 1
HIST_ROWS = POOL_HIST + 1
HGRN_CHUNK = 64
HGRN_RIDE_ROWS = 1024
MOBA_BLOCK = 256
MOBA_TOPK = 3
PAGE_SIZE = 128
PAGES_PER_BLOCK = MOBA_BLOCK // PAGE_SIZE
NORM_EPS = 1e-6
MASKED = -1e30
LOG2E = math.log2(math.e)
MXU_DEPTH = 256
VMEM_LIMIT = 48 * 1024 * 1024
_NT = (((1,), (1,)), ((), ()))


def _pick(n, candidates):
    for c in candidates:
        if n % c == 0:
            return c
    raise ValueError(f"no tile for {n}")


def _params(*sem):
    return pltpu.CompilerParams(dimension_semantics=sem, vmem_limit_bytes=VMEM_LIMIT)


def _silu(x):
    return x / (1.0 + jnp.exp(-x))


def _rms_rows(x, g):
    ms = jnp.mean(x * x, axis=-1, keepdims=True)
    return x * lax.rsqrt(ms + NORM_EPS) * g


def _inproj_kernel(x_ref, g_ref, w_ref, z_ref, h_ref):
    @pl.when(pl.program_id(1) == 0)
    def _():
        h_ref[...] = _rms_rows(x_ref[...], g_ref[...]).astype(BF16)
    z_ref[...] = jnp.dot(h_ref[...], w_ref[...], preferred_element_type=F32)


def _inproj(x2d, gain, w_all, layer):
    m, d = x2d.shape
    n = w_all.shape[2]
    tm = _pick(m, (1024, 512, 256, 128, 64, 32, 16, 8))
    tn = _pick(n, (1024, 512, 256, 128))
    return pl.pallas_call(
        _inproj_kernel,
        grid=(m // tm, n // tn),
        in_specs=[pl.BlockSpec((tm, d), lambda i, j: (i, 0)),
                  pl.BlockSpec((1, d), lambda i, j: (0, 0)),
                  pl.BlockSpec((None, d, tn), lambda i, j: (layer, 0, j))],
        out_specs=pl.BlockSpec((tm, tn), lambda i, j: (i, j)),
        out_shape=jax.ShapeDtypeStruct((m, n), F32),
        scratch_shapes=[pltpu.VMEM((tm, d), BF16)],
        compiler_params=_params("arbitrary", "arbitrary"),
        name="inproj",
    )(x2d, gain, w_all)


def _outproj_kernel(x_ref, a_ref, b_ref, c_ref, w_ref, y_ref):
    acc, row0 = None, 0
    for o_ref in (a_ref, b_ref, c_ref):
        part = jnp.dot(o_ref[...], w_ref[row0:row0 + o_ref.shape[1], :], preferred_element_type=F32)
        acc = part if acc is None else acc + part
        row0 += o_ref.shape[1]
    y_ref[...] = x_ref[...] + acc


def _outproj(x2d, oa, ob, oc, w_all, layer):
    m, d = x2d.shape
    d_mix = w_all.shape[1]
    assert oa.shape[1] + ob.shape[1] + oc.shape[1] == d_mix
    wide = 2 * d_mix * d * 2 <= VMEM_LIMIT // 2
    tn = d if wide else _pick(d, (512, 256, 128))
    tm = _pick(m, (512, 256, 128, 64, 32, 16, 8) if wide else (1024, 512, 256, 128, 64, 32, 16, 8))
    row = lambda a: pl.BlockSpec((tm, a.shape[1]), lambda i, j: (i, 0))
    return pl.pallas_call(
        _outproj_kernel,
        grid=(m // tm, d // tn),
        in_specs=[pl.BlockSpec((tm, tn), lambda i, j: (i, j)), row(oa), row(ob), row(oc),
                  pl.BlockSpec((None, d_mix, tn), lambda i, j: (layer, 0, j))],
        out_specs=pl.BlockSpec((tm, tn), lambda i, j: (i, j)),
        out_shape=jax.ShapeDtypeStruct((m, d), F32),
        compiler_params=_params("arbitrary", "arbitrary"),
        name="outproj",
    )(x2d, oa, ob, oc, w_all)


def _pool_tile(u_ref, g_ref, w_ref, scale_ref, o_ref, hist_out_ref, ext_ref, pos_first, tl, group):
    ext_ref[HIST_ROWS:HIST_ROWS + tl, :] = u_ref[...]
    pos = pos_first + lax.broadcasted_iota(jnp.int32, (tl, group), 0)
    for gi, w in enumerate(POOL_WINDOWS):
        cs = slice(gi * group, (gi + 1) * group)
        acc = ext_ref[HIST_ROWS:HIST_ROWS + tl, cs]
        for j in range(1, w):
            acc = acc + ext_ref[HIST_ROWS - j:HIST_ROWS - j + tl, cs]
        cnt = jnp.minimum(w, pos + 1).astype(F32)
        diff = acc / cnt - u_ref[:, cs]
        out = jnp.dot(diff.astype(BF16), w_ref[gi], preferred_element_type=F32) * scale_ref[:, cs]
        o_ref[:, cs] = (_silu(g_ref[:, cs]) * out).astype(BF16)
    hist_out_ref[...] = ext_ref[tl:tl + HIST_ROWS, :]


def _pool_kernel(u_ref, g_ref, hist_ref, w_ref, scale_ref, o_ref, hist_out_ref, ext_ref,
                 *, tl, pos0, group):
    t = pl.program_id(1)

    @pl.when(t == 0)
    def _():
        ext_ref[0:HIST_ROWS, :] = hist_ref[...]

    @pl.when(t > 0)
    def _():
        ext_ref[0:HIST_ROWS, :] = ext_ref[tl:tl + HIST_ROWS, :]

    _pool_tile(u_ref, g_ref, w_ref, scale_ref, o_ref, hist_out_ref, ext_ref, pos0 + t * tl, tl, group)


def _pool(z3, hist16, w_bf16, scale, pos0, d_pool):
    b, l, _ = z3.shape
    group = d_pool // len(POOL_WINDOWS)
    tl = _pick(l, (512, 256, 128, 64, 32, 16, 8))
    kern = functools.partial(_pool_kernel, tl=tl, pos0=pos0, group=group)
    return pl.pallas_call(
        kern,
        grid=(b, l // tl),
        in_specs=[pl.BlockSpec((None, tl, d_pool), lambda i, t: (i, t, 0)),
                  pl.BlockSpec((None, tl, d_pool), lambda i, t: (i, t, 1)),
                  pl.BlockSpec((None, HIST_ROWS, d_pool), lambda i, t: (i, 0, 0)),
                  pl.BlockSpec((len(POOL_WINDOWS), group, group), lambda i, t: (0, 0, 0)),
                  pl.BlockSpec((1, d_pool), lambda i, t: (0, 0))],
        out_specs=[pl.BlockSpec((None, tl, d_pool), lambda i, t: (i, t, 0)),
                   pl.BlockSpec((None, HIST_ROWS, d_pool), lambda i, t: (i, 0, 0))],
        out_shape=[jax.ShapeDtypeStruct((b, l, d_pool), BF16),
                   jax.ShapeDtypeStruct((b, HIST_ROWS, d_pool), F32)],
        scratch_shapes=[pltpu.VMEM((HIST_ROWS + tl, d_pool), F32)],
        compiler_params=_params("arbitrary", "arbitrary"),
        name="pool",
    )(z3, z3, hist16, w_bf16, scale)


def _cumsum_rows(x):
    n = x.shape[0]
    row = lax.broadcasted_iota(jnp.int32, x.shape, 0)
    s = 1
    while s < n:
        x = x + jnp.where(row >= s, pltpu.roll(x, s, axis=0), 0.0)
        s *= 2
    return x


def _hgrn_chunk(q_raw, z, v, lb_terms, st_ref, consts, chunk, sub, live=None):
    log_lb, log1m_lb, one_m_lb = lb_terms
    ones_kk, diag_mask, col_blk = consts
    lsig = jnp.minimum(z, 0.0) - jnp.log1p(jnp.exp(-jnp.abs(z)))
    b_ = log1m_lb + lsig
    logf = jnp.maximum(log_lb, b_) + jnp.log1p(jnp.exp(-jnp.abs(log_lb - b_)))
    k = one_m_lb * (1.0 / (1.0 + jnp.exp(z)))
    if live is not None:
        logf = jnp.where(live, logf, 0.0)
        k = jnp.where(live, k, 0.0)
    q = _silu(q_raw)
    cum = _cumsum_rows(logf)
    last = cum[chunk - 1:chunk, :]
    st = st_ref[...]
    o = lax.dot_general((q * jnp.exp(cum)).astype(BF16), st.astype(BF16),
                        (((1,), (1,)), ((), ())), preferred_element_type=F32)
    nsub = chunk // sub
    vb16 = v.astype(BF16)
    if nsub > 1:
        atts = [jnp.zeros((sub, chunk), F32)]
        for i in range(1, nsub):
            c_i = cum[i * sub - 1:i * sub, :]
            a = q[i * sub:(i + 1) * sub] * jnp.exp(cum[i * sub:(i + 1) * sub] - c_i)
            bmat = k * jnp.exp(jnp.minimum(c_i - cum, 0.0))
            att = lax.dot_general(a.astype(BF16), bmat.astype(BF16),
                                  (((1,), (1,)), ((), ())), preferred_element_type=F32)
            atts.append(jnp.where(col_blk < i, att, 0.0))
        att_all = jnp.concatenate(atts, axis=0)
        o = o + jnp.dot(att_all.astype(BF16), vb16, preferred_element_type=F32)
    diag = []
    for i in range(nsub):
        sl = slice(i * sub, (i + 1) * sub)
        cb, kb, qb, vb = cum[sl], k[sl], q[sl], v[sl]
        ws = []
        for t in range(sub):
            e = jnp.exp(jnp.minimum(cb[t:t + 1, :] - cb, 0.0))
            ws.append((qb[t:t + 1, :] * kb) * e)
        w = jnp.concatenate(ws, axis=0)
        att_rep = jnp.dot(w.astype(BF16), ones_kk, preferred_element_type=F32)
        p = jnp.where(diag_mask, att_rep * jnp.concatenate([vb] * sub, axis=0), 0.0)
        diag.append(jnp.sum(p.reshape(sub, sub, HEAD_DIM), axis=1))
    o = o + (jnp.concatenate(diag, axis=0) if nsub > 1 else diag[0])
    kd = (k * jnp.exp(last - cum)).astype(BF16)
    upd = lax.dot_general(vb16, kd, (((0,), (0,)), ((), ())), preferred_element_type=F32)
    st_ref[...] = st * jnp.exp(last) + upd
    return o


def _hgrn_consts(chunk, sub):
    ones_kk = jnp.ones((HEAD_DIM, HEAD_DIM), BF16)
    r = lax.broadcasted_iota(jnp.int32, (sub * sub, HEAD_DIM), 0)
    diag_mask = (r % sub) <= (r // sub)
    col_blk = lax.broadcasted_iota(jnp.int32, (sub, chunk), 1) // sub
    return ones_kk, diag_mask, col_blk


def _lb_terms(lb):
    return jnp.log(lb), jnp.log1p(-lb), 1.0 - lb


def _hgrn_halves(chunk):
    return [chunk >> (i + 1) for i in range(chunk.bit_length() - 1)]


def _hgrn_level_consts(chunk):
    halves = _hgrn_halves(chunk)
    nl = len(halves)
    m = np.zeros((nl + 1, chunk, chunk), np.float32)
    mask = np.zeros((nl, chunk, chunk), np.float32)
    for li, hz in enumerate(halves):
        for t in range(chunk):
            base = (t // (2 * hz)) * 2 * hz
            mid = base + hz
            if t >= mid:
                m[li, t, mid:t + 1] = 1.0
                mask[li, t, base:mid] = 1.0
            else:
                m[li, t, t + 1:mid] = 1.0
    m[nl] = np.tril(np.ones((chunk, chunk), np.float32))
    m3 = np.zeros(((nl + 1) * chunk, MXU_DEPTH), np.float32)
    m3[:, :3 * chunk] = np.tile(m.reshape((nl + 1) * chunk, chunk), (1, 3))
    return jnp.asarray(m3, BF16), jnp.asarray(mask, F32)


def _hgrn_prompt_consts(lb_ref, ng_ref, chunk):
    row = lax.broadcasted_iota(jnp.int32, (chunk, HEAD_DIM), 0)
    uppers = [(row % (2 * hz)) >= hz for hz in _hgrn_halves(chunk)]
    ones_kk = jnp.ones((HEAD_DIM, HEAD_DIM), BF16)
    kpad = jnp.zeros((MXU_DEPTH - 3 * chunk, HEAD_DIM), BF16)
    return _lb_terms(lb_ref[...]), ng_ref[...], uppers, ones_kk, kpad


def _run_phases(*phase_iters):
    live = list(phase_iters)
    while live:
        for it in list(live):
            if next(it, StopIteration) is StopIteration:
                live.remove(it)


def _hgrn_group(q_ref, f_ref, i_ref, g_ref, m3_ref, mask_ref, o_ref, st_ref, consts, rows, chunk):
    (log_lb, log1m_lb, one_m_lb), ng, uppers, ones_kk, kpad = consts
    nl = len(uppers)
    group = len(rows)
    qs, ks, vs, e2s = [], [], [], []
    for gi in range(group):
        z = f_ref[rows[gi], :]
        lsig = jnp.minimum(z, 0.0) - jnp.log1p(jnp.exp(-jnp.abs(z)))
        b_ = log1m_lb + lsig
        logf = jnp.maximum(log_lb, b_) + jnp.log1p(jnp.exp(-jnp.abs(log_lb - b_)))
        ks.append(one_m_lb * (1.0 / (1.0 + jnp.exp(z))))
        qs.append(_silu(q_ref[rows[gi], :]))
        vs.append(i_ref[rows[gi], :])
        g2 = logf * LOG2E
        hi = g2.astype(BF16)
        r1 = g2 - hi.astype(F32)
        mid = r1.astype(BF16)
        lo = (r1 - mid.astype(F32)).astype(BF16)
        e2s.append(jnp.dot(m3_ref[...], jnp.concatenate([hi, mid, lo, kpad], axis=0),
                           preferred_element_type=F32))
    yield
    atts = []
    for gi in range(group):
        att = None
        for li in range(nl):
            p = jnp.exp2(e2s[gi][li * chunk:(li + 1) * chunk])
            ab = (jnp.where(uppers[li], qs[gi], ks[gi]) * p).astype(BF16)
            a = lax.dot_general(ab, ab, _NT, preferred_element_type=F32) * mask_ref[li]
            att = a if att is None else att + a
        atts.append(att)
    yield
    outs, kds, qes, dec = [], [], [], []
    for gi in range(group):
        q, k, v = qs[gi], ks[gi], vs[gi]
        cum2 = e2s[gi][nl * chunk:(nl + 1) * chunk]
        last2 = cum2[chunk - 1:chunk, :]
        o = jnp.dot(atts[gi].astype(BF16), v.astype(BF16), preferred_element_type=F32)
        o = o + jnp.dot((q * k).astype(BF16), ones_kk, preferred_element_type=F32) * v
        outs.append(o)
        qes.append((q * jnp.exp2(cum2)).astype(BF16))
        kds.append((k * jnp.exp2(last2 - cum2)).astype(BF16))
        dec.append(jnp.exp2(last2))
    upds = [lax.dot_general(vs[gi].astype(BF16), kds[gi], (((0,), (0,)), ((), ())),
                            preferred_element_type=F32) for gi in range(group)]
    yield
    st = st_ref[...]
    for gi in range(group):
        o = outs[gi] + lax.dot_general(qes[gi], st.astype(BF16), _NT, preferred_element_type=F32)
        o_ref[rows[gi], :] = (_silu(g_ref[rows[gi], :]) * _rms_rows(o, ng)).astype(BF16)
        st = st * dec[gi] + upds[gi]
    st_ref[...] = st


def _hgrn_prompt_kernel(q_ref, f_ref, i_ref, g_ref, lb_ref, ng_ref, s0_ref, m3_ref, mask_ref,
                        o_ref, s_out_ref, st_ref, *, l, chunk, group):
    st_ref[...] = s0_ref[...].T
    consts = _hgrn_prompt_consts(lb_ref, ng_ref, chunk)

    def body(c, carry):
        rows = [pl.ds(pl.multiple_of(c * (chunk * group) + gi * chunk, chunk), chunk) for gi in range(group)]
        _run_phases(_hgrn_group(q_ref, f_ref, i_ref, g_ref, m3_ref, mask_ref, o_ref, st_ref, consts, rows, chunk))
        return carry

    lax.fori_loop(0, l // (chunk * group), body, 0)
    s_out_ref[...] = st_ref[...].T


def _hgrn_prompt(z3, lb, ng, s0, col0, n_heads):
    b, l, _ = z3.shape
    chunk = HGRN_CHUNK
    group = _pick(l // chunk, (16, 8, 4, 2, 1))
    m3, mask = _hgrn_level_consts(chunk)
    kern = functools.partial(_hgrn_prompt_kernel, l=l, chunk=chunk, group=group)
    zspec = lambda k: pl.BlockSpec((None, l, HEAD_DIM), lambda i, h: (i, 0, col0 + k * n_heads + h))
    hspec = pl.BlockSpec((1, HEAD_DIM), lambda i, h: (0, h))
    sspec = pl.BlockSpec((None, None, HEAD_DIM, HEAD_DIM), lambda i, h: (i, h, 0, 0))
    return pl.pallas_call(
        kern,
        grid=(b, n_heads),
        in_specs=[zspec(0), zspec(1), zspec(2), zspec(3), hspec, hspec, sspec,
                  pl.BlockSpec(m3.shape, lambda i, h: (0, 0)),
                  pl.BlockSpec(mask.shape, lambda i, h: (0, 0, 0))],
        out_specs=[pl.BlockSpec((None, l, HEAD_DIM), lambda i, h: (i, 0, h)), sspec],
        out_shape=[jax.ShapeDtypeStruct((b, l, n_heads * HEAD_DIM), BF16),
                   jax.ShapeDtypeStruct((b, n_heads, HEAD_DIM, HEAD_DIM), F32)],
        scratch_shapes=[pltpu.VMEM((HEAD_DIM, HEAD_DIM), F32)],
        compiler_params=_params("arbitrary", "arbitrary"),
        name="hgrn_prompt",
    )(z3, z3, z3, z3, lb, ng, s0, m3, mask)


def _hgrn_sample_kernel(z_ref, lb_ref, ng_ref, s0_ref, o_ref, s_out_ref, st_ref,
                        *, l, chunk, col0, n_heads):
    consts = _hgrn_consts(chunk, chunk)
    live = lax.broadcasted_iota(jnp.int32, (chunk, HEAD_DIM), 0) < l
    pad = jnp.zeros((chunk - l, HEAD_DIM), F32)
    col = lambda k, h: slice((col0 + k * n_heads + h) * HEAD_DIM, (col0 + k * n_heads + h + 1) * HEAD_DIM)
    head = lambda h: slice(h * HEAD_DIM, (h + 1) * HEAD_DIM)
    padded = lambda k, h: jnp.concatenate([z_ref[:, col(k, h)], pad], axis=0)
    for h in range(n_heads):
        st_ref[...] = s0_ref[h].T
        o = _hgrn_chunk(padded(0, h), padded(1, h), padded(2, h), _lb_terms(lb_ref[:, head(h)]), st_ref, consts,
                        chunk, chunk, live=live)
        o = _silu(z_ref[:, col(3, h)]) * _rms_rows(o[0:l], ng_ref[:, head(h)])
        o_ref[:, head(h)] = o.astype(BF16)
        s_out_ref[h] = st_ref[...].T


def _hgrn_sample(z3, lb, ng, s0_all, layer, col0, n_heads):
    b, l, ncols = z3.shape
    chunk = 16
    kern = functools.partial(_hgrn_sample_kernel, l=l, chunk=chunk, col0=col0, n_heads=n_heads)
    d = n_heads * HEAD_DIM
    sspec = pl.BlockSpec((None, n_heads, HEAD_DIM, HEAD_DIM), lambda i: (i, 0, 0, 0))
    return pl.pallas_call(
        kern,
        grid=(b,),
        in_specs=[pl.BlockSpec((None, l, ncols), lambda i: (i, 0, 0)),
                  pl.BlockSpec((1, d), lambda i: (0, 0)),
                  pl.BlockSpec((1, d), lambda i: (0, 0)),
                  pl.BlockSpec((None, None, n_heads, HEAD_DIM, HEAD_DIM), lambda i: (layer, i, 0, 0, 0))],
        out_specs=[pl.BlockSpec((None, l, d), lambda i: (i, 0, 0)), sspec],
        out_shape=[jax.ShapeDtypeStruct((b, l, d), BF16),
                   jax.ShapeDtypeStruct((b, n_heads, HEAD_DIM, HEAD_DIM), F32)],
        scratch_shapes=[pltpu.VMEM((HEAD_DIM, HEAD_DIM), F32)],
        compiler_params=_params("arbitrary"),
        name="hgrn_sample",
    )(z3, lb, ng, s0_all)


def _topk_select(sc, n_cand):
    lane = lax.broadcasted_iota(jnp.int32, sc.shape, 1)
    rank = jnp.zeros(sc.shape, F32)
    for m in range(n_cand):
        sc_m = sc[:, m:m + 1]
        ge = jnp.where(sc_m >= sc, 1.0, 0.0)
        gt = jnp.where(sc_m > sc, 1.0, 0.0)
        rank = rank + jnp.where(lane > m, ge, gt)
    return jnp.where(rank < MOBA_TOPK, 1.0, 0.0)


def _moba_prompt_kernel(slope_ref, q_ref, k_ref, v_ref, g_ref, qg_ref, kg_ref, *rest, l, nb, n_prev):
    if n_prev:
        kprev_ref, vprev_ref = rest[:2]
        rest = rest[2:]
    o_ref, ko_ref, vo_ref, qn_ref, qa_ref, ka_ref, vb_ref, km_ref = rest
    blk, d = MOBA_BLOCK, HEAD_DIM
    nbp = -(-nb // 8) * 8
    slope2 = slope_ref[pl.program_id(1)] * LOG2E
    if n_prev:
        ko_ref[0:n_prev] = kprev_ref[...]
        vo_ref[0:n_prev] = vprev_ref[...]
    qn = _rms_rows(q_ref[...], qg_ref[...])
    qn_ref[...] = qn
    qa_ref[:, 0:d] = (qn * (d ** -0.5 * LOG2E)).astype(BF16)
    kn = _rms_rows(k_ref[...], kg_ref[...])
    ka_ref[:, 0:d] = kn.astype(BF16)
    key_blk = lax.broadcasted_iota(jnp.int32, (l, d), 0) // blk
    ka_ref[:, d:2 * d] = jnp.where(key_blk == lax.broadcasted_iota(jnp.int32, (l, d), 1), 1.0, 0.0).astype(BF16)
    ko_ref[n_prev] = kn.reshape(l // PAGE_SIZE, PAGE_SIZE, d)
    v = v_ref[...]
    vb_ref[...] = v.astype(BF16)
    vo_ref[n_prev] = v.reshape(l // PAGE_SIZE, PAGE_SIZE, d)
    km_ref[...] = jnp.zeros(km_ref.shape, F32)
    for n in range(nb):
        kblk = ko_ref[n_prev, n * PAGES_PER_BLOCK:(n + 1) * PAGES_PER_BLOCK].reshape(blk, d)
        km_ref[n:n + 1, :] = jnp.mean(kblk, axis=0, keepdims=True)
    causal = lax.broadcasted_iota(jnp.int32, (blk, blk), 1) <= lax.broadcasted_iota(jnp.int32, (blk, blk), 0)
    blk_row = lax.broadcasted_iota(jnp.int32, (nbp, blk), 0)

    def scores(i):
        rows = slice(i * blk, (i + 1) * blk)
        nk = (i + 1) * blk
        if i <= MOBA_TOPK:
            return lax.dot_general(qa_ref[rows, 0:d], ka_ref[0:nk, 0:d], _NT, preferred_element_type=F32)
        sct = lax.dot_general(km_ref[0:nbp, :], qn_ref[rows, :], _NT, preferred_element_type=F32,
                              precision=lax.Precision.HIGHEST)
        rank = jnp.zeros((nbp, blk), F32)
        for m in range(i):
            sm = sct[m:m + 1, :]
            ge = jnp.where(sm >= sct, 1.0, 0.0)
            gt = jnp.where(sm > sct, 1.0, 0.0)
            rank = rank + jnp.where(blk_row > m, ge, gt)
        drop = jnp.where((rank < MOBA_TOPK) | (blk_row >= i), 0.0, MASKED)
        drop = jnp.concatenate([drop, jnp.zeros((d - nbp, blk), F32)], axis=0).T
        qa_ref[rows, d:2 * d] = drop.astype(BF16)
        return lax.dot_general(qa_ref[rows, :], ka_ref[0:nk, :], _NT, preferred_element_type=F32)

    s_next = scores(0)
    for i in range(nb):
        rows = slice(i * blk, (i + 1) * blk)
        n_past = i * blk
        s = s_next
        if i + 1 < nb:
            s_next = scores(i + 1)
        kpos = lax.broadcasted_iota(jnp.int32, (1, n_past + blk), 1) - n_past
        s = s + slope2 * kpos.astype(F32)
        s_own = jnp.where(causal, s[:, n_past:], MASKED)
        m = jnp.max(s_own, axis=-1, keepdims=True)
        if i:
            s_past = s[:, :n_past]
            m = jnp.maximum(m, jnp.max(s_past, axis=-1, keepdims=True))
        p_own = jnp.exp2(s_own - m)
        den = jnp.sum(p_own, axis=-1, keepdims=True)
        o = jnp.dot(p_own.astype(BF16), vb_ref[n_past:n_past + blk, :], preferred_element_type=F32)
        if i:
            p_past = jnp.exp2(s_past - m)
            den = den + jnp.sum(p_past, axis=-1, keepdims=True)
            o = o + jnp.dot(p_past.astype(BF16), vb_ref[0:n_past, :], preferred_element_type=F32)
        o_ref[rows, :] = (_silu(g_ref[rows, :]) * (o / den)).astype(BF16)


def _moba_prompt(z3, qg, kg, slopes, col0, n_heads, prev=None):
    b, l, _ = z3.shape
    assert l % MOBA_BLOCK == 0
    nb = l // MOBA_BLOCK
    assert nb <= HEAD_DIM
    npg = l // PAGE_SIZE
    n_prev = 0 if prev is None else prev[0].shape[0]
    kern = functools.partial(_moba_prompt_kernel, l=l, nb=nb, n_prev=n_prev)
    zspec = lambda k: pl.BlockSpec((None, l, HEAD_DIM), lambda i, h: (i, 0, col0 + k * n_heads + h))
    gspec = pl.BlockSpec((1, HEAD_DIM), lambda i, h: (0, 0))
    pspec = lambda n: pl.BlockSpec((n, None, npg, None, PAGE_SIZE, HEAD_DIM), lambda i, h: (0, i, 0, h, 0, 0))
    page_shape = jax.ShapeDtypeStruct((n_prev + 1, b, npg, n_heads, PAGE_SIZE, HEAD_DIM), F32)
    prev_specs = [pspec(n_prev)] * 2 if n_prev else []
    return pl.pallas_call(
        kern,
        grid=(b, n_heads),
        in_specs=[pl.BlockSpec(memory_space=pltpu.SMEM),
                  zspec(0), zspec(1), zspec(2), zspec(3), gspec, gspec] + prev_specs,
        out_specs=[pl.BlockSpec((None, l, HEAD_DIM), lambda i, h: (i, 0, h)), pspec(n_prev + 1), pspec(n_prev + 1)],
        out_shape=[jax.ShapeDtypeStruct((b, l, n_heads * HEAD_DIM), BF16), page_shape, page_shape],
        scratch_shapes=[pltpu.VMEM((l, HEAD_DIM), F32),
                        pltpu.VMEM((l, 2 * HEAD_DIM), BF16),
                        pltpu.VMEM((l, 2 * HEAD_DIM), BF16),
                        pltpu.VMEM((l, HEAD_DIM), BF16),
                        pltpu.VMEM((HEAD_DIM, HEAD_DIM), F32)],
        compiler_params=_params("arbitrary", "arbitrary"),
        name="moba_prompt",
    )(jnp.asarray(slopes, F32), z3, z3, z3, z3, qg, kg, *(prev or ()))


def _sample_parts(R, *, l, pp, n_heads, nb, col0, past, slopes):
    z_ref, qg_ref, kg_ref, k_refs, v_refs = R.z_ref, R.qg_ref, R.kg_ref, R.k_refs, R.v_refs
    o_ref, ko_ref, vo_ref = R.o_ref, R.ko_ref, R.vo_ref
    qn_ref, wq_ref, m_ref, l_ref, acc_ref, km_ref = R.qn_ref, R.wq_ref, R.m_ref, R.l_ref, R.acc_ref, R.km_ref
    blk = MOBA_BLOCK
    qscale = HEAD_DIM ** -0.5 * LOG2E
    col = lambda k, h: slice((col0 + k * n_heads + h) * HEAD_DIM, (col0 + k * n_heads + h + 1) * HEAD_DIM)
    zpad = jnp.zeros((HEAD_DIM - l, HEAD_DIM), F32)

    def init():
        for h in range(n_heads):
            qn = _rms_rows(z_ref[:, col(0, h)], qg_ref[...])
            qn_ref[h] = qn
            qt = jnp.concatenate([qn * qscale, zpad], axis=0).T
            if h:
                qt = pltpu.roll(qt, h * l, axis=1)
            wq_ref[h * HEAD_DIM:(h + 1) * HEAD_DIM, :] = qt.astype(BF16)
            ko_ref[h] = _rms_rows(z_ref[:, col(1, h)], kg_ref[...])
            vo_ref[h] = z_ref[:, col(2, h)]
        km_ref[...] = jnp.zeros(km_ref.shape, F32)
        m_ref[...] = jnp.zeros(m_ref.shape, F32)
        l_ref[...] = jnp.zeros(l_ref.shape, F32)

    def step(j):
        lane = lax.broadcasted_iota(jnp.int32, (1, HEAD_DIM), 1)
        slope2 = jnp.zeros((1, HEAD_DIM), F32)
        for h in range(n_heads):
            slope2 = jnp.where(lane // l == h, slopes[h] * LOG2E, slope2)
        lq = (lane % l).astype(F32)
        key_bias = slope2 * lax.broadcasted_iota(jnp.int32, (blk, HEAD_DIM), 0).astype(F32)
        bps = pp // PAGES_PER_BLOCK
        pages = lambda bi: range(bi * PAGES_PER_BLOCK, (bi + 1) * PAGES_PER_BLOCK)
        scores = []
        for bi in range(bps):
            kcat = jnp.concatenate(
                [jnp.concatenate([k_refs[p][h].astype(BF16) for h in range(n_heads)], axis=1) for p in pages(bi)],
                axis=0)
            scores.append(jnp.dot(kcat, wq_ref[...], preferred_element_type=F32))
        yield
        probs = []
        for bi in range(bps):
            n = j * bps + bi
            s = scores[bi] + key_bias + slope2 * ((n * blk - past).astype(F32) - lq)
            m = jnp.max(s, axis=0, keepdims=True)
            p_ = jnp.exp2(s - m)
            m_ref[pl.ds(n, 1), :] = m
            l_ref[pl.ds(n, 1), :] = jnp.sum(p_, axis=0, keepdims=True)
            probs.append(p_.T)
            for h in range(n_heads):
                kblk = jnp.concatenate([k_refs[p][h] for p in pages(bi)], axis=0)
                km_ref[h, pl.ds(n, 1), :] = jnp.mean(kblk, axis=0, keepdims=True)
        yield
        for bi in range(bps):
            n = j * bps + bi
            for h in range(n_heads):
                vblk = jnp.concatenate([v_refs[p][h].astype(BF16) for p in pages(bi)], axis=0)
                acc_ref[n, h * l:(h + 1) * l, :] = jnp.dot(probs[bi][h * l:(h + 1) * l, :].astype(BF16), vblk,
                                                           preferred_element_type=F32)

    def final():
        hl = n_heads * l
        ro = lax.broadcasted_iota(jnp.int32, (hl, HEAD_DIM), 0)
        co = lax.broadcasted_iota(jnp.int32, (hl, HEAD_DIM), 1)
        qi = ro % l
        slope_rows = jnp.zeros((hl, HEAD_DIM), F32)
        for h in range(n_heads):
            slope_rows = jnp.where(ro // l == h, slopes[h] * LOG2E, slope_rows)
        sc, s, vown = [], [], []
        for h in range(n_heads):
            qn = qn_ref[h]
            sc.append(lax.dot_general(qn, km_ref[h], _NT, preferred_element_type=F32,
                                      precision=lax.Precision.HIGHEST))
            kown = jnp.concatenate([ko_ref[h], zpad], axis=0).astype(BF16)
            vown.append(jnp.concatenate([vo_ref[h], zpad], axis=0).astype(BF16))
            s.append(lax.dot_general((qn * qscale).astype(BF16), kown, _NT, preferred_element_type=F32))
        keep = (_topk_select(jnp.concatenate(sc, axis=0), nb) > 0.5) & (co < nb)
        mh = m_ref[...].T[0:hl, :]
        lh = l_ref[...].T[0:hl, :]
        s = jnp.concatenate(s, axis=0)
        s = jnp.where(co <= qi, s - slope_rows * (qi - co).astype(F32), MASKED)
        mx = jnp.maximum(jnp.max(s, axis=-1, keepdims=True),
                         jnp.max(jnp.where(keep, mh, MASKED), axis=-1, keepdims=True))
        w = jnp.where(keep, jnp.exp2(mh - mx), 0.0)
        p_ = jnp.exp2(s - mx)
        den = jnp.sum(p_, axis=-1, keepdims=True) + jnp.sum(w * lh, axis=-1, keepdims=True)
        num = jnp.concatenate(
            [jnp.dot(p_[h * l:(h + 1) * l].astype(BF16), vown[h], preferred_element_type=F32)
             for h in range(n_heads)], axis=0)
        for n in range(nb):
            num = num + w[:, n:n + 1] * acc_ref[n]
        o = num / den
        for h in range(n_heads):
            hs = slice(h * HEAD_DIM, (h + 1) * HEAD_DIM)
            o_ref[:, hs] = (_silu(z_ref[:, col(3, h)]) * o[h * l:(h + 1) * l]).astype(BF16)

    return init, step, final


def _unpack_sample_refs(z_ref, qg_ref, kg_ref, pages, outs, scratch, pp):
    return types.SimpleNamespace(
        z_ref=z_ref, qg_ref=qg_ref, kg_ref=kg_ref, k_refs=pages[:pp], v_refs=pages[pp:],
        o_ref=outs[0], ko_ref=outs[1], vo_ref=outs[2],
        qn_ref=scratch[0], wq_ref=scratch[1], m_ref=scratch[2], l_ref=scratch[3], acc_ref=scratch[4],
        km_ref=scratch[5])


def _moba_sample_kernel(pt_ref, z_ref, qg_ref, kg_ref, *rest, pp, **static):
    del pt_ref
    refs = _unpack_sample_refs(z_ref, qg_ref, kg_ref, rest[:2 * pp], rest[2 * pp:2 * pp + 3], rest[2 * pp + 3:], pp)
    init, step, final = _sample_parts(refs, pp=pp, **static)
    j = pl.program_id(1)
    pl.when(j == 0)(init)
    _run_phases(step(j))
    pl.when(j == pl.num_programs(1) - 1)(final)


def _sample_hgrn_kernel(pt_ref, z_ref, qg_ref, kg_ref, *rest, pp, chunk, hsteps, n_hsteps, n_psteps, **static):
    del pt_ref
    hq_ref, hf_ref, hi_ref, hg_ref, lb_ref, ng_ref, s0_ref, m3_ref, mask_ref = rest[2 * pp:2 * pp + 9]
    pu_ref, pg_ref, pw_ref, pscale_ref = rest[2 * pp + 9:2 * pp + 13]
    outs = rest[2 * pp + 13:2 * pp + 20]
    scratch = rest[2 * pp + 20:]
    ho_ref, hs_ref, po_ref, phist_ref = outs[3:7]
    st_ref, ext_ref = scratch[6:8]
    refs = _unpack_sample_refs(z_ref, qg_ref, kg_ref, rest[:2 * pp], outs[:3], scratch[:6], pp)
    init, step, final = _sample_parts(refs, pp=pp, **static)
    j = pl.program_id(1)
    s = pl.program_id(0) * pl.num_programs(1) + j
    in_hgrn = s < n_hsteps
    in_pool = jnp.logical_and(s >= n_hsteps, s < n_hsteps + n_psteps)
    t = s % hsteps
    tp = (s - n_hsteps) % hsteps
    tl = ho_ref.shape[0]
    rows = [pl.ds(gi * chunk, chunk) for gi in range(tl // chunk)]
    pl.when(j == 0)(init)

    @pl.when(in_hgrn & (t == 0))
    def _():
        st_ref[...] = s0_ref[...].T

    @pl.when(in_pool & (tp == 0))
    def _():
        ext_ref[0:HIST_ROWS, :] = jnp.zeros((HIST_ROWS, ext_ref.shape[1]), F32)

    @pl.when(in_pool & (tp > 0))
    def _():
        ext_ref[0:HIST_ROWS, :] = ext_ref[tl:tl + HIST_ROWS, :]

    @pl.when(in_hgrn)
    def _():
        consts = _hgrn_prompt_consts(lb_ref, ng_ref, chunk)
        _run_phases(_hgrn_group(hq_ref, hf_ref, hi_ref, hg_ref, m3_ref, mask_ref, ho_ref, st_ref, consts, rows, chunk),
                    step(j))

    @pl.when(in_pool)
    def _():
        _run_phases(step(j))
        _pool_tile(pu_ref, pg_ref, pw_ref, pscale_ref, po_ref, phist_ref, ext_ref, tp * tl, tl,
                   pu_ref.shape[1] // len(POOL_WINDOWS))

    @pl.when(s >= n_hsteps + n_psteps)
    def _():
        _run_phases(step(j))

    @pl.when(in_hgrn & (t == hsteps - 1))
    def _():
        hs_ref[...] = st_ref[...].T

    pl.when(j == pl.num_programs(1) - 1)(final)


def _sample_page_group(n_pages):
    return _pick(n_pages, (16, 8, 4, 2))


def _can_ride(b, l, n_heads, sb, n_pages):
    if l % HGRN_RIDE_ROWS:
        return False
    return b * (n_heads + 1) * (l // HGRN_RIDE_ROWS) <= sb * (n_pages // _sample_page_group(n_pages))


def _moba_sample(z3, qg, kg, cache_k, cache_v, page_table, layer, slopes, col0, n_heads, ride=None):
    b, l, ncols = z3.shape
    n_pages = page_table.shape[1]
    assert n_pages % PAGES_PER_BLOCK == 0, "the own block must hold no cached page"
    nb = n_pages // PAGES_PER_BLOCK
    assert nb <= HEAD_DIM
    assert n_heads * l <= HEAD_DIM
    pp = _sample_page_group(n_pages)
    nj = n_pages // pp
    static = dict(l=l, pp=pp, n_heads=n_heads, nb=nb, col0=col0, past=n_pages * PAGE_SIZE, slopes=slopes)
    d = n_heads * HEAD_DIM
    gspec = pl.BlockSpec((1, HEAD_DIM), lambda i, j, pt: (0, 0))

    def page_spec(p):
        return pl.BlockSpec((None, None, n_heads, PAGE_SIZE, HEAD_DIM),
                            lambda i, j, pt: (layer, pt[i, j * pp + p], 0, 0, 0))

    new_spec = pl.BlockSpec((None, n_heads, l, HEAD_DIM), lambda i, j, pt: (i, 0, 0, 0))
    new_shape = jax.ShapeDtypeStruct((b, n_heads, l, HEAD_DIM), F32)
    stat = pltpu.VMEM((HEAD_DIM, HEAD_DIM), F32)
    in_specs = ([pl.BlockSpec((None, l, ncols), lambda i, j, pt: (i, 0, 0)), gspec, gspec]
                + [page_spec(p) for p in range(pp)] * 2)
    out_specs = [pl.BlockSpec((None, l, d), lambda i, j, pt: (i, 0, 0)), new_spec, new_spec]
    out_shape = [jax.ShapeDtypeStruct((b, l, d), BF16), new_shape, new_shape]
    scratch = [pltpu.VMEM((n_heads, l, HEAD_DIM), F32),
               pltpu.VMEM((n_heads * HEAD_DIM, HEAD_DIM), BF16),
               stat, stat,
               pltpu.VMEM((nb, n_heads * l, HEAD_DIM), F32),
               pltpu.VMEM((n_heads, HEAD_DIM, HEAD_DIM), F32)]
    args = [page_table, z3, qg, kg, *([cache_k] * pp), *([cache_v] * pp)]
    if ride is None:
        kern = functools.partial(_moba_sample_kernel, **static)
        name = "moba_sample"
    else:
        zp, lb, ng, s0, hcol0, hh, w_pool, pool_scale, d_pool = ride
        pb, pl_rows, _ = zp.shape
        assert _can_ride(pb, pl_rows, hh, b, n_pages)
        tl = HGRN_RIDE_ROWS
        hsteps = pl_rows // tl
        n_hsteps = pb * hh * hsteps
        n_psteps = pb * hsteps
        m3, mask = _hgrn_level_consts(HGRN_CHUNK)

        def pool_tile(i, j):
            q = jnp.clip(i * nj + j - n_hsteps, 0, n_psteps - 1)
            return q // hsteps, q % hsteps

        def where(i, j):
            s = jnp.minimum(i * nj + j, n_hsteps - 1)
            return (s // hsteps) // hh, (s // hsteps) % hh, s % hsteps

        def zspec(k):
            def index(i, j, pt):
                bp, h, t = where(i, j)
                return bp, t, hcol0 + k * hh + h
            return pl.BlockSpec((None, tl, HEAD_DIM), index)

        hspec = pl.BlockSpec((1, HEAD_DIM), lambda i, j, pt: (0, where(i, j)[1]))
        sspec = pl.BlockSpec((None, None, HEAD_DIM, HEAD_DIM),
                             lambda i, j, pt: (where(i, j)[0], where(i, j)[1], 0, 0))
        ospec = pl.BlockSpec((None, tl, HEAD_DIM),
                             lambda i, j, pt: (where(i, j)[0], where(i, j)[2], where(i, j)[1]))
        pool_in = lambda c: pl.BlockSpec((None, tl, d_pool), lambda i, j, pt: (*pool_tile(i, j), c))
        hist_spec = pl.BlockSpec((None, HIST_ROWS, d_pool), lambda i, j, pt: (pool_tile(i, j)[0], 0, 0))
        in_specs += [zspec(0), zspec(1), zspec(2), zspec(3), hspec, hspec, sspec,
                     pl.BlockSpec(m3.shape, lambda i, j, pt: (0, 0)),
                     pl.BlockSpec(mask.shape, lambda i, j, pt: (0, 0, 0)),
                     pool_in(0), pool_in(1),
                     pl.BlockSpec(w_pool.shape, lambda i, j, pt: (0, 0, 0)),
                     pl.BlockSpec((1, d_pool), lambda i, j, pt: (0, 0))]
        out_specs += [ospec, sspec, pool_in(0), hist_spec]
        out_shape += [jax.ShapeDtypeStruct((pb, pl_rows, hh * HEAD_DIM), BF16),
                      jax.ShapeDtypeStruct((pb, hh, HEAD_DIM, HEAD_DIM), F32),
                      jax.ShapeDtypeStruct((pb, pl_rows, d_pool), BF16),
                      jax.ShapeDtypeStruct((pb, HIST_ROWS, d_pool), F32)]
        scratch += [pltpu.VMEM((HEAD_DIM, HEAD_DIM), F32), pltpu.VMEM((HIST_ROWS + tl, d_pool), F32)]
        args += [zp, zp, zp, zp, lb, ng, s0, m3, mask, zp, zp, w_pool, pool_scale]
        kern = functools.partial(_sample_hgrn_kernel, chunk=HGRN_CHUNK, hsteps=hsteps, n_hsteps=n_hsteps,
                                 n_psteps=n_psteps, **static)
        name = "moba_sample_prompt_riders"
    return pl.pallas_call(
        kern,
        grid_spec=pltpu.PrefetchScalarGridSpec(
            num_scalar_prefetch=1, grid=(b, nj), in_specs=in_specs, out_specs=out_specs,
            scratch_shapes=scratch),
        out_shape=out_shape,
        compiler_params=_params("arbitrary", "arbitrary"),
        name=name,
    )(*args)


def _alibi_slopes(n):
    def geo(m):
        start = 2.0 ** (-8.0 / m)
        return [start ** (i + 1) for i in range(m)]
    if (n & (n - 1)) == 0:
        return geo(n)
    c = 2 ** int(math.floor(math.log2(n)))
    return geo(c) + geo(2 * c)[0::2][: n - c]


def kernel(x_prompt, x_sample, cache_k, cache_v, state_hgrn, state_pool, page_table, norm_gain, w_in,
           pool_w, pool_scale, hgrn_lb, hgrn_norm_gain, q_norm_gain, k_norm_gain, w_out):
    depth, d, _ = w_in.shape
    d_pool = d // 4
    d_hgrn = (d - d_pool) // 2
    d_att = d - d_pool - d_hgrn
    h_hgrn, h_att = d_hgrn // HEAD_DIM, d_att // HEAD_DIM
    col_hgrn = 2 * d_pool // HEAD_DIM
    col_att = col_hgrn + 4 * h_hgrn
    b, t, _ = x_prompt.shape
    sb, sl, _ = x_sample.shape
    past = page_table.shape[1] * PAGE_SIZE
    slopes = [float(np.float32(s)) for s in _alibi_slopes(h_att)]
    lb_cum = jnp.cumsum(jax.nn.softmax(hgrn_lb.astype(F32), axis=0), axis=0)
    lb_all = lb_cum - lb_cum[:1]
    zero_hist = jnp.zeros((b, HIST_ROWS, d_pool), F32)
    zero_state = jnp.zeros((b, h_hgrn, HEAD_DIM, HEAD_DIM), F32)

    y_p = x_prompt.reshape(b * t, d)
    y_s = x_sample.reshape(sb * sl, d)
    outs = [[] for _ in range(6)]
    kv_pages = None
    w_i = w_in.astype(BF16)
    w_o = w_out.astype(BF16)
    for layer in range(depth):
        w_p = pool_w[layer].astype(BF16)
        gain = norm_gain[layer][None]
        scale = pool_scale[layer][None]
        lb, ng = lb_all[layer][None], hgrn_norm_gain[layer][None]
        qg, kg = q_norm_gain[layer][None], k_norm_gain[layer][None]

        z = _inproj(y_p, gain, w_i, layer).reshape(b, t, -1)
        zs = _inproj(y_s, gain, w_i, layer).reshape(sb, sl, -1)
        sample_args = (zs, qg, kg, cache_k, cache_v, page_table, layer, slopes, col_att, h_att)
        if _can_ride(b, t, h_hgrn, sb, page_table.shape[1]):
            oc_s, k_s, v_s, ob, s_p, oa, hist_p = _moba_sample(
                *sample_args, ride=(z, lb, ng, zero_state, col_hgrn, h_hgrn, w_p, scale, d_pool))
        else:
            oc_s, k_s, v_s = _moba_sample(*sample_args)
            ob, s_p = _hgrn_prompt(z, lb, ng, zero_state, col_hgrn, h_hgrn)
            oa, hist_p = _pool(z, zero_hist, w_p, scale, 0, d_pool)
        oc, *kv_pages = _moba_prompt(z, qg, kg, slopes, col_att, h_att, prev=kv_pages)
        y_p = _outproj(y_p, oa.reshape(b * t, -1), ob.reshape(b * t, -1), oc.reshape(b * t, -1), w_o, layer)

        hist16 = jnp.pad(state_pool[layer], ((0, 0), (HIST_ROWS - POOL_HIST, 0), (0, 0)))
        oa, hist_s = _pool(zs, hist16, w_p, scale, past, d_pool)
        ob, s_s = _hgrn_sample(zs, lb, ng, state_hgrn, layer, col_hgrn, h_hgrn)
        y_s = _outproj(y_s, oa.reshape(sb * sl, -1), ob.reshape(sb * sl, -1), oc_s.reshape(sb * sl, -1), w_o, layer)

        for lst, a in zip(outs, (s_p, hist_p[:, 1:], k_s, v_s, s_s, hist_s[:, 1:])):
            lst.append(a)
    s_p, hist_p, k_s, v_s, s_s, hist_s = (jnp.stack(o) for o in outs)
    return (y_p.reshape(b, t, d), y_s.reshape(sb, sl, d), kv_pages[0], kv_pages[1], s_p, hist_p,
            k_s, v_s, s_s, hist_s)
```

```python
import functools
import math
import types

import jax
import jax.numpy as jnp
import numpy as np
from jax import lax
from jax.experimental import pallas as pl
from jax.experimental.pallas import tpu as pltpu

F32 = jnp.float32
BF16 = jnp.bfloat16

HEAD_DIM = 128
POOL_WINDOWS = (2, 4, 8, 16)
POOL_HIST = max(POOL_WINDOWS) - 1
HIST_ROWS = POOL_HIST + 1
HGRN_CHUNK = 64
HGRN_RIDE_ROWS = 1024
MOBA_BLOCK = 256
MOBA_TOPK = 3
PAGE_SIZE = 128
PAGES_PER_BLOCK = MOBA_BLOCK // PAGE_SIZE
NORM_EPS = 1e-6
MASKED = -1e30
LOG2E = math.log2(math.e)
MXU_DEPTH = 256
VMEM_LIMIT = 48 * 1024 * 1024
_NT = (((1,), (1,)), ((), ()))


def _pick(n, candidates):
    for c in candidates:
        if n % c == 0:
            return c
    raise ValueError(f"no tile for {n}")


def _params(*sem):
    return pltpu.CompilerParams(dimension_semantics=sem, vmem_limit_bytes=VMEM_LIMIT)


def _silu(x):
    return x / (1.0 + jnp.exp(-x))


def _rms_rows(x, g):
    ms = jnp.mean(x * x, axis=-1, keepdims=True)
    return x * lax.rsqrt(ms + NORM_EPS) * g


def _inproj_kernel(x_ref, g_ref, w_ref, z_ref, h_ref):
    @pl.when(pl.program_id(1) == 0)
    def _():
        h_ref[...] = _rms_rows(x_ref[...], g_ref[...]).astype(BF16)
    z_ref[...] = jnp.dot(h_ref[...], w_ref[...], preferred_element_type=F32)


def _inproj(x2d, gain, w_all, layer):
    m, d = x2d.shape
    n = w_all.shape[2]
    tm = _pick(m, (1024, 512, 256, 128, 64, 32, 16, 8))
    tn = _pick(n, (1024, 512, 256, 128))
    return pl.pallas_call(
        _inproj_kernel,
        grid=(m // tm, n // tn),
        in_specs=[pl.BlockSpec((tm, d), lambda i, j: (i, 0)),
                  pl.BlockSpec((1, d), lambda i, j: (0, 0)),
                  pl.BlockSpec((None, d, tn), lambda i, j: (layer, 0, j))],
        out_specs=pl.BlockSpec((tm, tn), lambda i, j: (i, j)),
        out_shape=jax.ShapeDtypeStruct((m, n), F32),
        scratch_shapes=[pltpu.VMEM((tm, d), BF16)],
        compiler_params=_params("arbitrary", "arbitrary"),
        name="inproj",
    )(x2d, gain, w_all)


def _inproj_cast_kernel(x_ref, g_ref, w_ref, z_ref, wb_ref, h_ref):
    @pl.when(pl.program_id(0) == 0)
    def _():
        h_ref[...] = _rms_rows(x_ref[...], g_ref[...]).astype(BF16)
    wb = w_ref[...].astype(BF16)
    wb_ref[...] = wb
    z_ref[...] = jnp.dot(h_ref[...], wb, preferred_element_type=F32)


def _inproj_cast(x2d, gain, w_f32_all, layer):
    m, d = x2d.shape
    n = w_f32_all.shape[2]
    tn = _pick(n, (1024, 512, 256, 128))
    return pl.pallas_call(
        _inproj_cast_kernel,
        grid=(n // tn,),
        in_specs=[pl.BlockSpec((m, d), lambda j: (0, 0)),
                  pl.BlockSpec((1, d), lambda j: (0, 0)),
                  pl.BlockSpec((None, d, tn), lambda j: (layer, 0, j))],
        out_specs=[pl.BlockSpec((m, tn), lambda j: (0, j)),
                   pl.BlockSpec((None, d, tn), lambda j: (0, 0, j))],
        out_shape=[jax.ShapeDtypeStruct((m, n), F32), jax.ShapeDtypeStruct((1, d, n), BF16)],
        scratch_shapes=[pltpu.VMEM((m, d), BF16)],
        compiler_params=_params("arbitrary"),
        name="inproj_cast",
    )(x2d, gain, w_f32_all)


def _outproj_kernel(x_ref, a_ref, b_ref, c_ref, w_ref, y_ref):
    acc, row0 = None, 0
    for o_ref in (a_ref, b_ref, c_ref):
        part = jnp.dot(o_ref[...], w_ref[row0:row0 + o_ref.shape[1], :], preferred_element_type=F32)
        acc = part if acc is None else acc + part
        row0 += o_ref.shape[1]
    y_ref[...] = x_ref[...] + acc


def _outproj(x2d, oa, ob, oc, w_all, layer):
    m, d = x2d.shape
    d_mix = w_all.shape[1]
    assert oa.shape[1] + ob.shape[1] + oc.shape[1] == d_mix
    wide = 2 * d_mix * d * 2 <= VMEM_LIMIT // 2
    tn = d if wide else _pick(d, (512, 256, 128))
    tm = _pick(m, (512, 256, 128, 64, 32, 16, 8) if wide else (1024, 512, 256, 128, 64, 32, 16, 8))
    row = lambda a: pl.BlockSpec((tm, a.shape[1]), lambda i, j: (i, 0))
    return pl.pallas_call(
        _outproj_kernel,
        grid=(m // tm, d // tn),
        in_specs=[pl.BlockSpec((tm, tn), lambda i, j: (i, j)), row(oa), row(ob), row(oc),
                  pl.BlockSpec((None, d_mix, tn), lambda i, j: (layer, 0, j))],
        out_specs=pl.BlockSpec((tm, tn), lambda i, j: (i, j)),
        out_shape=jax.ShapeDtypeStruct((m, d), F32),
        compiler_params=_params("arbitrary", "arbitrary"),
        name="outproj",
    )(x2d, oa, ob, oc, w_all)


def _pool_tile(u_ref, g_ref, w_ref, scale_ref, o_ref, hist_out_ref, ext_ref, pos_first, tl, group):
    ext_ref[HIST_ROWS:HIST_ROWS + tl, :] = u_ref[...]
    pos = pos_first + lax.broadcasted_iota(jnp.int32, (tl, group), 0)
    for gi, w in enumerate(POOL_WINDOWS):
        cs = slice(gi * group, (gi + 1) * group)
        acc = ext_ref[HIST_ROWS:HIST_ROWS + tl, cs]
        for j in range(1, w):
            acc = acc + ext_ref[HIST_ROWS - j:HIST_ROWS - j + tl, cs]
        cnt = jnp.minimum(w, pos + 1).astype(F32)
        diff = acc / cnt - u_ref[:, cs]
        out = jnp.dot(diff.astype(BF16), w_ref[gi], preferred_element_type=F32) * scale_ref[:, cs]
        o_ref[:, cs] = (_silu(g_ref[:, cs]) * out).astype(BF16)
    hist_out_ref[...] = ext_ref[tl:tl + HIST_ROWS, :]


def _pool_kernel(u_ref, g_ref, hist_ref, w_ref, scale_ref, o_ref, hist_out_ref, ext_ref,
                 *, tl, pos0, group):
    t = pl.program_id(1)

    @pl.when(t == 0)
    def _():
        ext_ref[0:HIST_ROWS, :] = hist_ref[...]

    @pl.when(t > 0)
    def _():
        ext_ref[0:HIST_ROWS, :] = ext_ref[tl:tl + HIST_ROWS, :]

    _pool_tile(u_ref, g_ref, w_ref, scale_ref, o_ref, hist_out_ref, ext_ref, pos0 + t * tl, tl, group)


def _pool(z3, hist16, w_bf16, scale, pos0, d_pool):
    b, l, _ = z3.shape
    group = d_pool // len(POOL_WINDOWS)
    tl = _pick(l, (512, 256, 128, 64, 32, 16, 8))
    kern = functools.partial(_pool_kernel, tl=tl, pos0=pos0, group=group)
    return pl.pallas_call(
        kern,
        grid=(b, l // tl),
        in_specs=[pl.BlockSpec((None, tl, d_pool), lambda i, t: (i, t, 0)),
                  pl.BlockSpec((None, tl, d_pool), lambda i, t: (i, t, 1)),
                  pl.BlockSpec((None, HIST_ROWS, d_pool), lambda i, t: (i, 0, 0)),
                  pl.BlockSpec((len(POOL_WINDOWS), group, group), lambda i, t: (0, 0, 0)),
                  pl.BlockSpec((1, d_pool), lambda i, t: (0, 0))],
        out_specs=[pl.BlockSpec((None, tl, d_pool), lambda i, t: (i, t, 0)),
                   pl.BlockSpec((None, HIST_ROWS, d_pool), lambda i, t: (i, 0, 0))],
        out_shape=[jax.ShapeDtypeStruct((b, l, d_pool), BF16),
                   jax.ShapeDtypeStruct((b, HIST_ROWS, d_pool), F32)],
        scratch_shapes=[pltpu.VMEM((HIST_ROWS + tl, d_pool), F32)],
        compiler_params=_params("arbitrary", "arbitrary"),
        name="pool",
    )(z3, z3, hist16, w_bf16, scale)


def _cumsum_rows(x):
    n = x.shape[0]
    row = lax.broadcasted_iota(jnp.int32, x.shape, 0)
    s = 1
    while s < n:
        x = x + jnp.where(row >= s, pltpu.roll(x, s, axis=0), 0.0)
        s *= 2
    return x


def _hgrn_chunk(q_raw, z, v, lb_terms, st_ref, consts, chunk, sub, live=None):
    log_lb, log1m_lb, one_m_lb = lb_terms
    ones_kk, diag_mask, col_blk = consts
    lsig = jnp.minimum(z, 0.0) - jnp.log1p(jnp.exp(-jnp.abs(z)))
    b_ = log1m_lb + lsig
    logf = jnp.maximum(log_lb, b_) + jnp.log1p(jnp.exp(-jnp.abs(log_lb - b_)))
    k = one_m_lb * (1.0 / (1.0 + jnp.exp(z)))
    if live is not None:
        logf = jnp.where(live, logf, 0.0)
        k = jnp.where(live, k, 0.0)
    q = _silu(q_raw)
    cum = _cumsum_rows(logf)
    last = cum[chunk - 1:chunk, :]
    st = st_ref[...]
    o = lax.dot_general((q * jnp.exp(cum)).astype(BF16), st.astype(BF16),
                        (((1,), (1,)), ((), ())), preferred_element_type=F32)
    nsub = chunk // sub
    vb16 = v.astype(BF16)
    if nsub > 1:
        atts = [jnp.zeros((sub, chunk), F32)]
        for i in range(1, nsub):
            c_i = cum[i * sub - 1:i * sub, :]
            a = q[i * sub:(i + 1) * sub] * jnp.exp(cum[i * sub:(i + 1) * sub] - c_i)
            bmat = k * jnp.exp(jnp.minimum(c_i - cum, 0.0))
            att = lax.dot_general(a.astype(BF16), bmat.astype(BF16),
                                  (((1,), (1,)), ((), ())), preferred_element_type=F32)
            atts.append(jnp.where(col_blk < i, att, 0.0))
        att_all = jnp.concatenate(atts, axis=0)
        o = o + jnp.dot(att_all.astype(BF16), vb16, preferred_element_type=F32)
    diag = []
    for i in range(nsub):
        sl = slice(i * sub, (i + 1) * sub)
        cb, kb, qb, vb = cum[sl], k[sl], q[sl], v[sl]
        ws = []
        for t in range(sub):
            e = jnp.exp(jnp.minimum(cb[t:t + 1, :] - cb, 0.0))
            ws.append((qb[t:t + 1, :] * kb) * e)
        w = jnp.concatenate(ws, axis=0)
        att_rep = jnp.dot(w.astype(BF16), ones_kk, preferred_element_type=F32)
        p = jnp.where(diag_mask, att_rep * jnp.concatenate([vb] * sub, axis=0), 0.0)
        diag.append(jnp.sum(p.reshape(sub, sub, HEAD_DIM), axis=1))
    o = o + (jnp.concatenate(diag, axis=0) if nsub > 1 else diag[0])
    kd = (k * jnp.exp(last - cum)).astype(BF16)
    upd = lax.dot_general(vb16, kd, (((0,), (0,)), ((), ())), preferred_element_type=F32)
    st_ref[...] = st * jnp.exp(last) + upd
    return o


def _hgrn_consts(chunk, sub):
    ones_kk = jnp.ones((HEAD_DIM, HEAD_DIM), BF16)
    r = lax.broadcasted_iota(jnp.int32, (sub * sub, HEAD_DIM), 0)
    diag_mask = (r % sub) <= (r // sub)
    col_blk = lax.broadcasted_iota(jnp.int32, (sub, chunk), 1) // sub
    return ones_kk, diag_mask, col_blk


def _lb_terms(lb):
    return jnp.log(lb), jnp.log1p(-lb), 1.0 - lb


def _hgrn_halves(chunk):
    return [chunk >> (i + 1) for i in range(chunk.bit_length() - 1)]


def _hgrn_level_consts(chunk):
    halves = _hgrn_halves(chunk)
    nl = len(halves)
    m = np.zeros((nl + 1, chunk, chunk), np.float32)
    mask = np.zeros((nl, chunk, chunk), np.float32)
    for li, hz in enumerate(halves):
        for t in range(chunk):
            base = (t // (2 * hz)) * 2 * hz
            mid = base + hz
            if t >= mid:
                m[li, t, mid:t + 1] = 1.0
                mask[li, t, base:mid] = 1.0
            else:
                m[li, t, t + 1:mid] = 1.0
    m[nl] = np.tril(np.ones((chunk, chunk), np.float32))
    m3 = np.zeros(((nl + 1) * chunk, MXU_DEPTH), np.float32)
    m3[:, :3 * chunk] = np.tile(m.reshape((nl + 1) * chunk, chunk), (1, 3))
    return jnp.asarray(m3, BF16), jnp.asarray(mask, F32)


def _hgrn_prompt_consts(lb_ref, ng_ref, chunk):
    row = lax.broadcasted_iota(jnp.int32, (chunk, HEAD_DIM), 0)
    uppers = [(row % (2 * hz)) >= hz for hz in _hgrn_halves(chunk)]
    ones_kk = jnp.ones((HEAD_DIM, HEAD_DIM), BF16)
    kpad = jnp.zeros((MXU_DEPTH - 3 * chunk, HEAD_DIM), BF16)
    return _lb_terms(lb_ref[...]), ng_ref[...], uppers, ones_kk, kpad


def _run_phases(*phase_iters):
    live = list(phase_iters)
    while live:
        for it in list(live):
            if next(it, StopIteration) is StopIteration:
                live.remove(it)


def _hgrn_group(q_ref, f_ref, i_ref, g_ref, m3_ref, mask_ref, o_ref, st_ref, consts, rows, chunk):
    (log_lb, log1m_lb, one_m_lb), ng, uppers, ones_kk, kpad = consts
    nl = len(uppers)
    group = len(rows)
    qs, ks, vs, e2s = [], [], [], []
    for gi in range(group):
        z = f_ref[rows[gi], :]
        lsig = jnp.minimum(z, 0.0) - jnp.log1p(jnp.exp(-jnp.abs(z)))
        b_ = log1m_lb + lsig
        logf = jnp.maximum(log_lb, b_) + jnp.log1p(jnp.exp(-jnp.abs(log_lb - b_)))
        ks.append(one_m_lb * (1.0 / (1.0 + jnp.exp(z))))
        qs.append(_silu(q_ref[rows[gi], :]))
        vs.append(i_ref[rows[gi], :])
        g2 = logf * LOG2E
        hi = g2.astype(BF16)
        r1 = g2 - hi.astype(F32)
        mid = r1.astype(BF16)
        lo = (r1 - mid.astype(F32)).astype(BF16)
        e2s.append(jnp.dot(m3_ref[...], jnp.concatenate([hi, mid, lo, kpad], axis=0),
                           preferred_element_type=F32))
    yield
    atts = []
    for gi in range(group):
        att = None
        for li in range(nl):
            p = jnp.exp2(e2s[gi][li * chunk:(li + 1) * chunk])
            ab = (jnp.where(uppers[li], qs[gi], ks[gi]) * p).astype(BF16)
            a = lax.dot_general(ab, ab, _NT, preferred_element_type=F32) * mask_ref[li]
            att = a if att is None else att + a
        atts.append(att)
    yield
    outs, kds, qes, dec = [], [], [], []
    for gi in range(group):
        q, k, v = qs[gi], ks[gi], vs[gi]
        cum2 = e2s[gi][nl * chunk:(nl + 1) * chunk]
        last2 = cum2[chunk - 1:chunk, :]
        o = jnp.dot(atts[gi].astype(BF16), v.astype(BF16), preferred_element_type=F32)
        o = o + jnp.dot((q * k).astype(BF16), ones_kk, preferred_element_type=F32) * v
        outs.append(o)
        qes.append((q * jnp.exp2(cum2)).astype(BF16))
        kds.append((k * jnp.exp2(last2 - cum2)).astype(BF16))
        dec.append(jnp.exp2(last2))
    upds = [lax.dot_general(vs[gi].astype(BF16), kds[gi], (((0,), (0,)), ((), ())),
                            preferred_element_type=F32) for gi in range(group)]
    yield
    st = st_ref[...]
    for gi in range(group):
        o = outs[gi] + lax.dot_general(qes[gi], st.astype(BF16), _NT, preferred_element_type=F32)
        o_ref[rows[gi], :] = (_silu(g_ref[rows[gi], :]) * _rms_rows(o, ng)).astype(BF16)
        st = st * dec[gi] + upds[gi]
    st_ref[...] = st


def _hgrn_prompt_kernel(q_ref, f_ref, i_ref, g_ref, lb_ref, ng_ref, s0_ref, m3_ref, mask_ref,
                        o_ref, s_out_ref, st_ref, *, l, chunk, group):
    st_ref[...] = s0_ref[...].T
    consts = _hgrn_prompt_consts(lb_ref, ng_ref, chunk)

    def body(c, carry):
        rows = [pl.ds(pl.multiple_of(c * (chunk * group) + gi * chunk, chunk), chunk) for gi in range(group)]
        _run_phases(_hgrn_group(q_ref, f_ref, i_ref, g_ref, m3_ref, mask_ref, o_ref, st_ref, consts, rows, chunk))
        return carry

    lax.fori_loop(0, l // (chunk * group), body, 0)
    s_out_ref[...] = st_ref[...].T


def _hgrn_prompt(z3, lb, ng, s0, col0, n_heads):
    b, l, _ = z3.shape
    chunk = HGRN_CHUNK
    group = _pick(l // chunk, (16, 8, 4, 2, 1))
    m3, mask = _hgrn_level_consts(chunk)
    kern = functools.partial(_hgrn_prompt_kernel, l=l, chunk=chunk, group=group)
    zspec = lambda k: pl.BlockSpec((None, l, HEAD_DIM), lambda i, h: (i, 0, col0 + k * n_heads + h))
    hspec = pl.BlockSpec((1, HEAD_DIM), lambda i, h: (0, h))
    sspec = pl.BlockSpec((None, None, HEAD_DIM, HEAD_DIM), lambda i, h: (i, h, 0, 0))
    return pl.pallas_call(
        kern,
        grid=(b, n_heads),
        in_specs=[zspec(0), zspec(1), zspec(2), zspec(3), hspec, hspec, sspec,
                  pl.BlockSpec(m3.shape, lambda i, h: (0, 0)),
                  pl.BlockSpec(mask.shape, lambda i, h: (0, 0, 0))],
        out_specs=[pl.BlockSpec((None, l, HEAD_DIM), lambda i, h: (i, 0, h)), sspec],
        out_shape=[jax.ShapeDtypeStruct((b, l, n_heads * HEAD_DIM), BF16),
                   jax.ShapeDtypeStruct((b, n_heads, HEAD_DIM, HEAD_DIM), F32)],
        scratch_shapes=[pltpu.VMEM((HEAD_DIM, HEAD_DIM), F32)],
        compiler_params=_params("arbitrary", "arbitrary"),
        name="hgrn_prompt",
    )(z3, z3, z3, z3, lb, ng, s0, m3, mask)


def _sample_mixers_kernel(z_ref, hist_ref, pw_ref, pscale_ref, lb_ref, ng_ref, s0_ref,
                          oa_ref, hist_out_ref, ob_ref, s_out_ref, ext_ref, st_ref,
                          *, l, chunk, col0, n_heads, pos0, d_pool, seqs):
    consts = _hgrn_consts(chunk, chunk)
    live = lax.broadcasted_iota(jnp.int32, (chunk, HEAD_DIM), 0) < l
    pad = jnp.zeros((chunk - l, HEAD_DIM), F32)
    col = lambda k, h: slice((col0 + k * n_heads + h) * HEAD_DIM, (col0 + k * n_heads + h + 1) * HEAD_DIM)
    head = lambda h: slice(h * HEAD_DIM, (h + 1) * HEAD_DIM)
    for s in range(seqs):
        ext_ref[0:HIST_ROWS, :] = hist_ref[s]
        _pool_tile(z_ref.at[s, :, 0:d_pool], z_ref.at[s, :, d_pool:2 * d_pool], pw_ref, pscale_ref,
                   oa_ref.at[s], hist_out_ref.at[s], ext_ref, pos0, l, d_pool // len(POOL_WINDOWS))
        padded = lambda k, h: jnp.concatenate([z_ref[s, :, col(k, h)], pad], axis=0)
        for h in range(n_heads):
            st_ref[...] = s0_ref[s, h].T
            o = _hgrn_chunk(padded(0, h), padded(1, h), padded(2, h), _lb_terms(lb_ref[:, head(h)]), st_ref,
                            consts, chunk, chunk, live=live)
            o = _silu(z_ref[s, :, col(3, h)]) * _rms_rows(o[0:l], ng_ref[:, head(h)])
            ob_ref[s, :, head(h)] = o.astype(BF16)
            s_out_ref[s, h] = st_ref[...].T


def _sample_mixers(z3, hist16, w_pool, pool_scale, pos0, d_pool, lb, ng, s0_all, layer, col0, n_heads):
    b, l, ncols = z3.shape
    seqs = _pick(b, (4, 2, 1))
    kern = functools.partial(_sample_mixers_kernel, l=l, chunk=16, col0=col0, n_heads=n_heads, pos0=pos0,
                             d_pool=d_pool, seqs=seqs)
    d = n_heads * HEAD_DIM
    rows = lambda w: pl.BlockSpec((seqs, l, w), lambda i: (i, 0, 0))
    hist_spec = pl.BlockSpec((seqs, HIST_ROWS, d_pool), lambda i: (i, 0, 0))
    sspec = pl.BlockSpec((seqs, n_heads, HEAD_DIM, HEAD_DIM), lambda i: (i, 0, 0, 0))
    return pl.pallas_call(
        kern,
        grid=(b // seqs,),
        in_specs=[rows(ncols), hist_spec,
                  pl.BlockSpec(w_pool.shape, lambda i: (0, 0, 0)),
                  pl.BlockSpec((1, d_pool), lambda i: (0, 0)),
                  pl.BlockSpec((1, d), lambda i: (0, 0)),
                  pl.BlockSpec((1, d), lambda i: (0, 0)),
                  pl.BlockSpec((None, seqs, n_heads, HEAD_DIM, HEAD_DIM), lambda i: (layer, i, 0, 0, 0))],
        out_specs=[rows(d_pool), hist_spec, rows(d), sspec],
        out_shape=[jax.ShapeDtypeStruct((b, l, d_pool), BF16),
                   jax.ShapeDtypeStruct((b, HIST_ROWS, d_pool), F32),
                   jax.ShapeDtypeStruct((b, l, d), BF16),
                   jax.ShapeDtypeStruct((b, n_heads, HEAD_DIM, HEAD_DIM), F32)],
        scratch_shapes=[pltpu.VMEM((HIST_ROWS + l, d_pool), F32), pltpu.VMEM((HEAD_DIM, HEAD_DIM), F32)],
        compiler_params=_params("arbitrary"),
        name="sample_mixers",
    )(z3, hist16, w_pool, pool_scale, lb, ng, s0_all)


def _topk_select(sc, n_cand):
    lane = lax.broadcasted_iota(jnp.int32, sc.shape, 1)
    rank = jnp.zeros(sc.shape, F32)
    for m in range(n_cand):
        sc_m = sc[:, m:m + 1]
        ge = jnp.where(sc_m >= sc, 1.0, 0.0)
        gt = jnp.where(sc_m > sc, 1.0, 0.0)
        rank = rank + jnp.where(lane > m, ge, gt)
    return jnp.where(rank < MOBA_TOPK, 1.0, 0.0)


def _moba_prompt_kernel(slope_ref, q_ref, k_ref, v_ref, g_ref, qg_ref, kg_ref, *rest, l, nb, n_prev):
    if n_prev:
        kprev_ref, vprev_ref = rest[:2]
        rest = rest[2:]
    o_ref, ko_ref, vo_ref, qn_ref, qa_ref, ka_ref, vb_ref, km_ref = rest
    blk, d = MOBA_BLOCK, HEAD_DIM
    nbp = -(-nb // 8) * 8
    slope2 = slope_ref[pl.program_id(1)] * LOG2E
    if n_prev:
        ko_ref[0:n_prev] = kprev_ref[...]
        vo_ref[0:n_prev] = vprev_ref[...]
    qn = _rms_rows(q_ref[...], qg_ref[...])
    qn_ref[...] = qn
    qa_ref[:, 0:d] = (qn * (d ** -0.5 * LOG2E)).astype(BF16)
    kn = _rms_rows(k_ref[...], kg_ref[...])
    ka_ref[:, 0:d] = kn.astype(BF16)
    key_blk = lax.broadcasted_iota(jnp.int32, (l, d), 0) // blk
    ka_ref[:, d:2 * d] = jnp.where(key_blk == lax.broadcasted_iota(jnp.int32, (l, d), 1), 1.0, 0.0).astype(BF16)
    ko_ref[n_prev] = kn.reshape(l // PAGE_SIZE, PAGE_SIZE, d)
    v = v_ref[...]
    vb_ref[...] = v.astype(BF16)
    vo_ref[n_prev] = v.reshape(l // PAGE_SIZE, PAGE_SIZE, d)
    km_ref[...] = jnp.zeros(km_ref.shape, F32)
    for n in range(nb):
        kblk = ko_ref[n_prev, n * PAGES_PER_BLOCK:(n + 1) * PAGES_PER_BLOCK].reshape(blk, d)
        km_ref[n:n + 1, :] = jnp.mean(kblk, axis=0, keepdims=True)
    causal = lax.broadcasted_iota(jnp.int32, (blk, blk), 1) <= lax.broadcasted_iota(jnp.int32, (blk, blk), 0)
    blk_row = lax.broadcasted_iota(jnp.int32, (nbp, blk), 0)

    def scores(i):
        rows = slice(i * blk, (i + 1) * blk)
        nk = (i + 1) * blk
        if i <= MOBA_TOPK:
            return lax.dot_general(qa_ref[rows, 0:d], ka_ref[0:nk, 0:d], _NT, preferred_element_type=F32)
        sct = lax.dot_general(km_ref[0:nbp, :], qn_ref[rows, :], _NT, preferred_element_type=F32,
                              precision=lax.Precision.HIGHEST)
        rank = jnp.zeros((nbp, blk), F32)
        for m in range(i):
            sm = sct[m:m + 1, :]
            ge = jnp.where(sm >= sct, 1.0, 0.0)
            gt = jnp.where(sm > sct, 1.0, 0.0)
            rank = rank + jnp.where(blk_row > m, ge, gt)
        drop = jnp.where((rank < MOBA_TOPK) | (blk_row >= i), 0.0, MASKED)
        drop = jnp.concatenate([drop, jnp.zeros((d - nbp, blk), F32)], axis=0).T
        qa_ref[rows, d:2 * d] = drop.astype(BF16)
        return lax.dot_general(qa_ref[rows, :], ka_ref[0:nk, :], _NT, preferred_element_type=F32)

    s_next = scores(0)
    for i in range(nb):
        rows = slice(i * blk, (i + 1) * blk)
        n_past = i * blk
        s = s_next
        if i + 1 < nb:
            s_next = scores(i + 1)
        kpos = lax.broadcasted_iota(jnp.int32, (1, n_past + blk), 1) - n_past
        s = s + slope2 * kpos.astype(F32)
        s_own = jnp.where(causal, s[:, n_past:], MASKED)
        m = jnp.max(s_own, axis=-1, keepdims=True)
        if i:
            s_past = s[:, :n_past]
            m = jnp.maximum(m, jnp.max(s_past, axis=-1, keepdims=True))
        p_own = jnp.exp2(s_own - m)
        den = jnp.sum(p_own, axis=-1, keepdims=True)
        o = jnp.dot(p_own.astype(BF16), vb_ref[n_past:n_past + blk, :], preferred_element_type=F32)
        if i:
            p_past = jnp.exp2(s_past - m)
            den = den + jnp.sum(p_past, axis=-1, keepdims=True)
            o = o + jnp.dot(p_past.astype(BF16), vb_ref[0:n_past, :], preferred_element_type=F32)
        o_ref[rows, :] = (_silu(g_ref[rows, :]) * (o / den)).astype(BF16)


def _moba_prompt(z3, qg, kg, slopes, col0, n_heads, prev=None):
    b, l, _ = z3.shape
    assert l % MOBA_BLOCK == 0
    nb = l // MOBA_BLOCK
    assert nb <= HEAD_DIM
    npg = l // PAGE_SIZE
    n_prev = 0 if prev is None else prev[0].shape[0]
    kern = functools.partial(_moba_prompt_kernel, l=l, nb=nb, n_prev=n_prev)
    zspec = lambda k: pl.BlockSpec((None, l, HEAD_DIM), lambda i, h: (i, 0, col0 + k * n_heads + h))
    gspec = pl.BlockSpec((1, HEAD_DIM), lambda i, h: (0, 0))
    pspec = lambda n: pl.BlockSpec((n, None, npg, None, PAGE_SIZE, HEAD_DIM), lambda i, h: (0, i, 0, h, 0, 0))
    page_shape = jax.ShapeDtypeStruct((n_prev + 1, b, npg, n_heads, PAGE_SIZE, HEAD_DIM), F32)
    prev_specs = [pspec(n_prev)] * 2 if n_prev else []
    return pl.pallas_call(
        kern,
        grid=(b, n_heads),
        in_specs=[pl.BlockSpec(memory_space=pltpu.SMEM),
                  zspec(0), zspec(1), zspec(2), zspec(3), gspec, gspec] + prev_specs,
        out_specs=[pl.BlockSpec((None, l, HEAD_DIM), lambda i, h: (i, 0, h)), pspec(n_prev + 1), pspec(n_prev + 1)],
        out_shape=[jax.ShapeDtypeStruct((b, l, n_heads * HEAD_DIM), BF16), page_shape, page_shape],
        scratch_shapes=[pltpu.VMEM((l, HEAD_DIM), F32),
                        pltpu.VMEM((l, 2 * HEAD_DIM), BF16),
                        pltpu.VMEM((l, 2 * HEAD_DIM), BF16),
                        pltpu.VMEM((l, HEAD_DIM), BF16),
                        pltpu.VMEM((HEAD_DIM, HEAD_DIM), F32)],
        compiler_params=_params("arbitrary", "arbitrary"),
        name="moba_prompt",
    )(jnp.asarray(slopes, F32), z3, z3, z3, z3, qg, kg, *(prev or ()))


def _sample_parts(R, *, l, pp, n_heads, nb, col0, past, slopes):
    z_ref, qg_ref, kg_ref, k_refs, v_refs = R.z_ref, R.qg_ref, R.kg_ref, R.k_refs, R.v_refs
    o_ref, ko_ref, vo_ref = R.o_ref, R.ko_ref, R.vo_ref
    qn_ref, wq_ref, m_ref, l_ref, acc_ref, km_ref = R.qn_ref, R.wq_ref, R.m_ref, R.l_ref, R.acc_ref, R.km_ref
    blk = MOBA_BLOCK
    qscale = HEAD_DIM ** -0.5 * LOG2E
    col = lambda k, h: slice((col0 + k * n_heads + h) * HEAD_DIM, (col0 + k * n_heads + h + 1) * HEAD_DIM)
    zpad = jnp.zeros((HEAD_DIM - l, HEAD_DIM), F32)

    def init():
        for h in range(n_heads):
            qn = _rms_rows(z_ref[:, col(0, h)], qg_ref[...])
            qn_ref[h] = qn
            qt = jnp.concatenate([qn * qscale, zpad], axis=0).T
            if h:
                qt = pltpu.roll(qt, h * l, axis=1)
            wq_ref[h * HEAD_DIM:(h + 1) * HEAD_DIM, :] = qt.astype(BF16)
            ko_ref[h] = _rms_rows(z_ref[:, col(1, h)], kg_ref[...])
            vo_ref[h] = z_ref[:, col(2, h)]
        km_ref[...] = jnp.zeros(km_ref.shape, F32)
        m_ref[...] = jnp.zeros(m_ref.shape, F32)
        l_ref[...] = jnp.zeros(l_ref.shape, F32)

    def step(j):
        lane = lax.broadcasted_iota(jnp.int32, (1, HEAD_DIM), 1)
        slope2 = jnp.zeros((1, HEAD_DIM), F32)
        for h in range(n_heads):
            slope2 = jnp.where(lane // l == h, slopes[h] * LOG2E, slope2)
        lq = (lane % l).astype(F32)
        key_bias = slope2 * lax.broadcasted_iota(jnp.int32, (blk, HEAD_DIM), 0).astype(F32)
        bps = pp // PAGES_PER_BLOCK
        pages = lambda bi: range(bi * PAGES_PER_BLOCK, (bi + 1) * PAGES_PER_BLOCK)
        scores = []
        for bi in range(bps):
            kcat = jnp.concatenate(
                [jnp.concatenate([k_refs[p][h].astype(BF16) for h in range(n_heads)], axis=1) for p in pages(bi)],
                axis=0)
            scores.append(jnp.dot(kcat, wq_ref[...], preferred_element_type=F32))
        yield
        probs = []
        for bi in range(bps):
            n = j * bps + bi
            s = scores[bi] + key_bias + slope2 * ((n * blk - past).astype(F32) - lq)
            m = jnp.max(s, axis=0, keepdims=True)
            p_ = jnp.exp2(s - m)
            m_ref[pl.ds(n, 1), :] = m
            l_ref[pl.ds(n, 1), :] = jnp.sum(p_, axis=0, keepdims=True)
            probs.append(p_.T)
            for h in range(n_heads):
                kblk = jnp.concatenate([k_refs[p][h] for p in pages(bi)], axis=0)
                km_ref[h, pl.ds(n, 1), :] = jnp.mean(kblk, axis=0, keepdims=True)
        yield
        for bi in range(bps):
            n = j * bps + bi
            for h in range(n_heads):
                vblk = jnp.concatenate([v_refs[p][h].astype(BF16) for p in pages(bi)], axis=0)
                acc_ref[n, h * l:(h + 1) * l, :] = jnp.dot(probs[bi][h * l:(h + 1) * l, :].astype(BF16), vblk,
                                                           preferred_element_type=F32)

    def final():
        hl = n_heads * l
        ro = lax.broadcasted_iota(jnp.int32, (hl, HEAD_DIM), 0)
        co = lax.broadcasted_iota(jnp.int32, (hl, HEAD_DIM), 1)
        qi = ro % l
        slope_rows = jnp.zeros((hl, HEAD_DIM), F32)
        for h in range(n_heads):
            slope_rows = jnp.where(ro // l == h, slopes[h] * LOG2E, slope_rows)
        sc, s, vown = [], [], []
        for h in range(n_heads):
            qn = qn_ref[h]
            sc.append(lax.dot_general(qn, km_ref[h], _NT, preferred_element_type=F32,
                                      precision=lax.Precision.HIGHEST))
            kown = jnp.concatenate([ko_ref[h], zpad], axis=0).astype(BF16)
            vown.append(jnp.concatenate([vo_ref[h], zpad], axis=0).astype(BF16))
            s.append(lax.dot_general((qn * qscale).astype(BF16), kown, _NT, preferred_element_type=F32))
        keep = (_topk_select(jnp.concatenate(sc, axis=0), nb) > 0.5) & (co < nb)
        mh = m_ref[...].T[0:hl, :]
        lh = l_ref[...].T[0:hl, :]
        s = jnp.concatenate(s, axis=0)
        s = jnp.where(co <= qi, s - slope_rows * (qi - co).astype(F32), MASKED)
        mx = jnp.maximum(jnp.max(s, axis=-1, keepdims=True),
                         jnp.max(jnp.where(keep, mh, MASKED), axis=-1, keepdims=True))
        w = jnp.where(keep, jnp.exp2(mh - mx), 0.0)
        p_ = jnp.exp2(s - mx)
        den = jnp.sum(p_, axis=-1, keepdims=True) + jnp.sum(w * lh, axis=-1, keepdims=True)
        num = jnp.concatenate(
            [jnp.dot(p_[h * l:(h + 1) * l].astype(BF16), vown[h], preferred_element_type=F32)
             for h in range(n_heads)], axis=0)
        for n in range(nb):
            num = num + w[:, n:n + 1] * acc_ref[n]
        o = num / den
        for h in range(n_heads):
            hs = slice(h * HEAD_DIM, (h + 1) * HEAD_DIM)
            o_ref[:, hs] = (_silu(z_ref[:, col(3, h)]) * o[h * l:(h + 1) * l]).astype(BF16)

    return init, step, final


def _unpack_sample_refs(z_ref, qg_ref, kg_ref, pages, outs, scratch, pp):
    return types.SimpleNamespace(
        z_ref=z_ref, qg_ref=qg_ref, kg_ref=kg_ref, k_refs=pages[:pp], v_refs=pages[pp:],
        o_ref=outs[0], ko_ref=outs[1], vo_ref=outs[2],
        qn_ref=scratch[0], wq_ref=scratch[1], m_ref=scratch[2], l_ref=scratch[3], acc_ref=scratch[4],
        km_ref=scratch[5])


def _moba_sample_kernel(pt_ref, z_ref, qg_ref, kg_ref, *rest, pp, **static):
    del pt_ref
    refs = _unpack_sample_refs(z_ref, qg_ref, kg_ref, rest[:2 * pp], rest[2 * pp:2 * pp + 3], rest[2 * pp + 3:], pp)
    init, step, final = _sample_parts(refs, pp=pp, **static)
    j = pl.program_id(1)
    pl.when(j == 0)(init)
    _run_phases(step(j))
    pl.when(j == pl.num_programs(1) - 1)(final)


def _sample_hgrn_kernel(pt_ref, z_ref, qg_ref, kg_ref, *rest, pp, chunk, hsteps, n_hsteps, n_psteps, **static):
    del pt_ref
    hq_ref, hf_ref, hi_ref, hg_ref, lb_ref, ng_ref, s0_ref, m3_ref, mask_ref = rest[2 * pp:2 * pp + 9]
    pu_ref, pg_ref, pw_ref, pscale_ref = rest[2 * pp + 9:2 * pp + 13]
    outs = rest[2 * pp + 13:2 * pp + 20]
    scratch = rest[2 * pp + 20:]
    ho_ref, hs_ref, po_ref, phist_ref = outs[3:7]
    st_ref, ext_ref = scratch[6:8]
    refs = _unpack_sample_refs(z_ref, qg_ref, kg_ref, rest[:2 * pp], outs[:3], scratch[:6], pp)
    init, step, final = _sample_parts(refs, pp=pp, **static)
    j = pl.program_id(1)
    s = pl.program_id(0) * pl.num_programs(1) + j
    in_hgrn = s < n_hsteps
    in_pool = jnp.logical_and(s >= n_hsteps, s < n_hsteps + n_psteps)
    t = s % hsteps
    tp = (s - n_hsteps) % hsteps
    tl = ho_ref.shape[0]
    rows = [pl.ds(gi * chunk, chunk) for gi in range(tl // chunk)]
    pl.when(j == 0)(init)

    @pl.when(in_hgrn & (t == 0))
    def _():
        st_ref[...] = s0_ref[...].T

    @pl.when(in_pool & (tp == 0))
    def _():
        ext_ref[0:HIST_ROWS, :] = jnp.zeros((HIST_ROWS, ext_ref.shape[1]), F32)

    @pl.when(in_pool & (tp > 0))
    def _():
        ext_ref[0:HIST_ROWS, :] = ext_ref[tl:tl + HIST_ROWS, :]

    @pl.when(in_hgrn)
    def _():
        consts = _hgrn_prompt_consts(lb_ref, ng_ref, chunk)
        _run_phases(_hgrn_group(hq_ref, hf_ref, hi_ref, hg_ref, m3_ref, mask_ref, ho_ref, st_ref, consts, rows, chunk),
                    step(j))

    @pl.when(in_pool)
    def _():
        _run_phases(step(j))
        _pool_tile(pu_ref, pg_ref, pw_ref, pscale_ref, po_ref, phist_ref, ext_ref, tp * tl, tl,
                   pu_ref.shape[1] // len(POOL_WINDOWS))

    @pl.when(s >= n_hsteps + n_psteps)
    def _():
        _run_phases(step(j))

    @pl.when(in_hgrn & (t == hsteps - 1))
    def _():
        hs_ref[...] = st_ref[...].T

    pl.when(j == pl.num_programs(1) - 1)(final)


def _sample_page_group(n_pages):
    return _pick(n_pages, (16, 8, 4, 2))


def _can_ride(b, l, n_heads, sb, n_pages):
    if l % HGRN_RIDE_ROWS:
        return False
    return b * (n_heads + 1) * (l // HGRN_RIDE_ROWS) <= sb * (n_pages // _sample_page_group(n_pages))


def _moba_sample(z3, qg, kg, cache_k, cache_v, page_table, layer, slopes, col0, n_heads, ride=None):
    b, l, ncols = z3.shape
    n_pages = page_table.shape[1]
    assert n_pages % PAGES_PER_BLOCK == 0, "the own block must hold no cached page"
    nb = n_pages // PAGES_PER_BLOCK
    assert nb <= HEAD_DIM
    assert n_heads * l <= HEAD_DIM
    pp = _sample_page_group(n_pages)
    nj = n_pages // pp
    static = dict(l=l, pp=pp, n_heads=n_heads, nb=nb, col0=col0, past=n_pages * PAGE_SIZE, slopes=slopes)
    d = n_heads * HEAD_DIM
    gspec = pl.BlockSpec((1, HEAD_DIM), lambda i, j, pt: (0, 0))

    def page_spec(p):
        return pl.BlockSpec((None, None, n_heads, PAGE_SIZE, HEAD_DIM),
                            lambda i, j, pt: (layer, pt[i, j * pp + p], 0, 0, 0))

    new_spec = pl.BlockSpec((None, n_heads, l, HEAD_DIM), lambda i, j, pt: (i, 0, 0, 0))
    new_shape = jax.ShapeDtypeStruct((b, n_heads, l, HEAD_DIM), F32)
    stat = pltpu.VMEM((HEAD_DIM, HEAD_DIM), F32)
    in_specs = ([pl.BlockSpec((None, l, ncols), lambda i, j, pt: (i, 0, 0)), gspec, gspec]
                + [page_spec(p) for p in range(pp)] * 2)
    out_specs = [pl.BlockSpec((None, l, d), lambda i, j, pt: (i, 0, 0)), new_spec, new_spec]
    out_shape = [jax.ShapeDtypeStruct((b, l, d), BF16), new_shape, new_shape]
    scratch = [pltpu.VMEM((n_heads, l, HEAD_DIM), F32),
               pltpu.VMEM((n_heads * HEAD_DIM, HEAD_DIM), BF16),
               stat, stat,
               pltpu.VMEM((nb, n_heads * l, HEAD_DIM), F32),
               pltpu.VMEM((n_heads, HEAD_DIM, HEAD_DIM), F32)]
    args = [page_table, z3, qg, kg, *([cache_k] * pp), *([cache_v] * pp)]
    if ride is None:
        kern = functools.partial(_moba_sample_kernel, **static)
        name = "moba_sample"
    else:
        zp, lb, ng, s0, hcol0, hh, w_pool, pool_scale, d_pool = ride
        pb, pl_rows, _ = zp.shape
        assert _can_ride(pb, pl_rows, hh, b, n_pages)
        tl = HGRN_RIDE_ROWS
        hsteps = pl_rows // tl
        n_hsteps = pb * hh * hsteps
        n_psteps = pb * hsteps
        m3, mask = _hgrn_level_consts(HGRN_CHUNK)

        def pool_tile(i, j):
            q = jnp.clip(i * nj + j - n_hsteps, 0, n_psteps - 1)
            return q // hsteps, q % hsteps

        def where(i, j):
            s = jnp.minimum(i * nj + j, n_hsteps - 1)
            return (s // hsteps) // hh, (s // hsteps) % hh, s % hsteps

        def zspec(k):
            def index(i, j, pt):
                bp, h, t = where(i, j)
                return bp, t, hcol0 + k * hh + h
            return pl.BlockSpec((None, tl, HEAD_DIM), index)

        hspec = pl.BlockSpec((1, HEAD_DIM), lambda i, j, pt: (0, where(i, j)[1]))
        sspec = pl.BlockSpec((None, None, HEAD_DIM, HEAD_DIM),
                             lambda i, j, pt: (where(i, j)[0], where(i, j)[1], 0, 0))
        ospec = pl.BlockSpec((None, tl, HEAD_DIM),
                             lambda i, j, pt: (where(i, j)[0], where(i, j)[2], where(i, j)[1]))
        pool_in = lambda c: pl.BlockSpec((None, tl, d_pool), lambda i, j, pt: (*pool_tile(i, j), c))
        hist_spec = pl.BlockSpec((None, HIST_ROWS, d_pool), lambda i, j, pt: (pool_tile(i, j)[0], 0, 0))
        in_specs += [zspec(0), zspec(1), zspec(2), zspec(3), hspec, hspec, sspec,
                     pl.BlockSpec(m3.shape, lambda i, j, pt: (0, 0)),
                     pl.BlockSpec(mask.shape, lambda i, j, pt: (0, 0, 0)),
                     pool_in(0), pool_in(1),
                     pl.BlockSpec(w_pool.shape, lambda i, j, pt: (0, 0, 0)),
                     pl.BlockSpec((1, d_pool), lambda i, j, pt: (0, 0))]
        out_specs += [ospec, sspec, pool_in(0), hist_spec]
        out_shape += [jax.ShapeDtypeStruct((pb, pl_rows, hh * HEAD_DIM), BF16),
                      jax.ShapeDtypeStruct((pb, hh, HEAD_DIM, HEAD_DIM), F32),
                      jax.ShapeDtypeStruct((pb, pl_rows, d_pool), BF16),
                      jax.ShapeDtypeStruct((pb, HIST_ROWS, d_pool), F32)]
        scratch += [pltpu.VMEM((HEAD_DIM, HEAD_DIM), F32), pltpu.VMEM((HIST_ROWS + tl, d_pool), F32)]
        args += [zp, zp, zp, zp, lb, ng, s0, m3, mask, zp, zp, w_pool, pool_scale]
        kern = functools.partial(_sample_hgrn_kernel, chunk=HGRN_CHUNK, hsteps=hsteps, n_hsteps=n_hsteps,
                                 n_psteps=n_psteps, **static)
        name = "moba_sample_prompt_riders"
    return pl.pallas_call(
        kern,
        grid_spec=pltpu.PrefetchScalarGridSpec(
            num_scalar_prefetch=1, grid=(b, nj), in_specs=in_specs, out_specs=out_specs,
            scratch_shapes=scratch),
        out_shape=out_shape,
        compiler_params=_params("arbitrary", "arbitrary"),
        name=name,
    )(*args)


def _alibi_slopes(n):
    def geo(m):
        start = 2.0 ** (-8.0 / m)
        return [start ** (i + 1) for i in range(m)]
    if (n & (n - 1)) == 0:
        return geo(n)
    c = 2 ** int(math.floor(math.log2(n)))
    return geo(c) + geo(2 * c)[0::2][: n - c]


def kernel(x_prompt, x_sample, cache_k, cache_v, state_hgrn, state_pool, page_table, norm_gain, w_in,
           pool_w, pool_scale, hgrn_lb, hgrn_norm_gain, q_norm_gain, k_norm_gain, w_out):
    depth, d, _ = w_in.shape
    d_pool = d // 4
    d_hgrn = (d - d_pool) // 2
    d_att = d - d_pool - d_hgrn
    h_hgrn, h_att = d_hgrn // HEAD_DIM, d_att // HEAD_DIM
    col_hgrn = 2 * d_pool // HEAD_DIM
    col_att = col_hgrn + 4 * h_hgrn
    b, t, _ = x_prompt.shape
    sb, sl, _ = x_sample.shape
    past = page_table.shape[1] * PAGE_SIZE
    slopes = [float(np.float32(s)) for s in _alibi_slopes(h_att)]
    lb_cum = jnp.cumsum(jax.nn.softmax(hgrn_lb.astype(F32), axis=0), axis=0)
    lb_all = lb_cum - lb_cum[:1]
    zero_hist = jnp.zeros((b, HIST_ROWS, d_pool), F32)
    zero_state = jnp.zeros((b, h_hgrn, HEAD_DIM, HEAD_DIM), F32)

    y_p = x_prompt.reshape(b * t, d)
    y_s = x_sample.reshape(sb * sl, d)
    outs = [[] for _ in range(6)]
    kv_pages = None
    cast_in_call = sb * sl <= 512
    w_i = None if cast_in_call else w_in.astype(BF16)
    w_o = w_out.astype(BF16)
    for layer in range(depth):
        w_p = pool_w[layer].astype(BF16)
        gain = norm_gain[layer][None]
        scale = pool_scale[layer][None]
        lb, ng = lb_all[layer][None], hgrn_norm_gain[layer][None]
        qg, kg = q_norm_gain[layer][None], k_norm_gain[layer][None]

        if cast_in_call:
            zs, w_l = _inproj_cast(y_s, gain, w_in, layer)
            z = _inproj(y_p, gain, w_l, 0).reshape(b, t, -1)
        else:
            zs = _inproj(y_s, gain, w_i, layer)
            z = _inproj(y_p, gain, w_i, layer).reshape(b, t, -1)
        zs = zs.reshape(sb, sl, -1)
        sample_args = (zs, qg, kg, cache_k, cache_v, page_table, layer, slopes, col_att, h_att)
        if _can_ride(b, t, h_hgrn, sb, page_table.shape[1]):
            oc_s, k_s, v_s, ob, s_p, oa, hist_p = _moba_sample(
                *sample_args, ride=(z, lb, ng, zero_state, col_hgrn, h_hgrn, w_p, scale, d_pool))
        else:
            oc_s, k_s, v_s = _moba_sample(*sample_args)
            ob, s_p = _hgrn_prompt(z, lb, ng, zero_state, col_hgrn, h_hgrn)
            oa, hist_p = _pool(z, zero_hist, w_p, scale, 0, d_pool)
        oc, *kv_pages = _moba_prompt(z, qg, kg, slopes, col_att, h_att, prev=kv_pages)
        y_p = _outproj(y_p, oa.reshape(b * t, -1), ob.reshape(b * t, -1), oc.reshape(b * t, -1), w_o, layer)

        hist16 = jnp.pad(state_pool[layer], ((0, 0), (HIST_ROWS - POOL_HIST, 0), (0, 0)))
        oa, hist_s, ob, s_s = _sample_mixers(zs, hist16, w_p, scale, past, d_pool, lb, ng, state_hgrn, layer,
                                             col_hgrn, h_hgrn)
        y_s = _outproj(y_s, oa.reshape(sb * sl, -1), ob.reshape(sb * sl, -1), oc_s.reshape(sb * sl, -1), w_o, layer)

        for lst, a in zip(outs, (s_p, hist_p[:, 1:], k_s, v_s, s_s, hist_s[:, 1:])):
            lst.append(a)
    s_p, hist_p, k_s, v_s, s_s, hist_s = (jnp.stack(o) for o in outs)
    return (y_p.reshape(b, t, d), y_s.reshape(sb, sl, d), kv_pages[0], kv_pages[1], s_p, hist_p,
            k_s, v_s, s_s, hist_s)
```

```python
import functools
import math
import types

import jax
import jax.numpy as jnp
import numpy as np
from jax import lax
from jax.experimental import pallas as pl
from jax.experimental.pallas import tpu as pltpu

F32 = jnp.float32
BF16 = jnp.bfloat16

HEAD_DIM = 128
POOL_WINDOWS = (2, 4, 8, 16)
POOL_HIST = max(POOL_WINDOWS) - 1
HIST_ROWS = POOL_HIST + 1
HGRN_CHUNK = 64
HGRN_RIDE_ROWS = 1024
MOBA_BLOCK = 256
MOBA_TOPK = 3
SOFTMAX_ROWS = 128
PAGE_SIZE = 128
PAGES_PER_BLOCK = MOBA_BLOCK // PAGE_SIZE
NORM_EPS = 1e-6
MASKED = -1e30
LOG2E = math.log2(math.e)
MXU_DEPTH = 256
VMEM_LIMIT = 48 * 1024 * 1024
_NT = (((1,), (1,)), ((), ()))


def _pick(n, candidates):
    for c in candidates:
        if n % c == 0:
            return c
    raise ValueError(f"no tile for {n}")


def _params(*sem):
    return pltpu.CompilerParams(dimension_semantics=sem, vmem_limit_bytes=VMEM_LIMIT)


def _silu(x):
    return x / (1.0 + jnp.exp(-x))


def _rms_rows(x, g):
    ms = jnp.mean(x * x, axis=-1, keepdims=True)
    return x * lax.rsqrt(ms + NORM_EPS) * g


def _inproj_kernel(x_ref, g_ref, w_ref, z_ref, h_ref):
    @pl.when(pl.program_id(1) == 0)
    def _():
        h_ref[...] = _rms_rows(x_ref[...], g_ref[...]).astype(BF16)
    z_ref[...] = jnp.dot(h_ref[...], w_ref[...], preferred_element_type=F32)


def _inproj(x2d, gain, w_all, layer):
    m, d = x2d.shape
    n = w_all.shape[2]
    tm = _pick(m, (1024, 512, 256, 128, 64, 32, 16, 8))
    tn = _pick(n, (1024, 512, 256, 128))
    return pl.pallas_call(
        _inproj_kernel,
        grid=(m // tm, n // tn),
        in_specs=[pl.BlockSpec((tm, d), lambda i, j: (i, 0)),
                  pl.BlockSpec((1, d), lambda i, j: (0, 0)),
                  pl.BlockSpec((None, d, tn), lambda i, j: (layer, 0, j))],
        out_specs=pl.BlockSpec((tm, tn), lambda i, j: (i, j)),
        out_shape=jax.ShapeDtypeStruct((m, n), F32),
        scratch_shapes=[pltpu.VMEM((tm, d), BF16)],
        compiler_params=_params("arbitrary", "arbitrary"),
        name="inproj",
    )(x2d, gain, w_all)


def _inproj_cast_kernel(x_ref, g_ref, w_ref, z_ref, wb_ref, h_ref):
    @pl.when(pl.program_id(0) == 0)
    def _():
        h_ref[...] = _rms_rows(x_ref[...], g_ref[...]).astype(BF16)
    wb = w_ref[...].astype(BF16)
    wb_ref[...] = wb
    z_ref[...] = jnp.dot(h_ref[...], wb, preferred_element_type=F32)


def _inproj_cast(x2d, gain, w_f32_all, layer):
    m, d = x2d.shape
    n = w_f32_all.shape[2]
    tn = _pick(n, (1024, 512, 256, 128))
    return pl.pallas_call(
        _inproj_cast_kernel,
        grid=(n // tn,),
        in_specs=[pl.BlockSpec((m, d), lambda j: (0, 0)),
                  pl.BlockSpec((1, d), lambda j: (0, 0)),
                  pl.BlockSpec((None, d, tn), lambda j: (layer, 0, j))],
        out_specs=[pl.BlockSpec((m, tn), lambda j: (0, j)),
                   pl.BlockSpec((None, d, tn), lambda j: (0, 0, j))],
        out_shape=[jax.ShapeDtypeStruct((m, n), F32), jax.ShapeDtypeStruct((1, d, n), BF16)],
        scratch_shapes=[pltpu.VMEM((m, d), BF16)],
        compiler_params=_params("arbitrary"),
        name="inproj_cast",
    )(x2d, gain, w_f32_all)


def _outproj_kernel(x_ref, a_ref, b_ref, c_ref, w_ref, y_ref):
    acc, row0 = None, 0
    for o_ref in (a_ref, b_ref, c_ref):
        part = jnp.dot(o_ref[...], w_ref[row0:row0 + o_ref.shape[1], :], preferred_element_type=F32)
        acc = part if acc is None else acc + part
        row0 += o_ref.shape[1]
    y_ref[...] = x_ref[...] + acc


def _outproj(x2d, oa, ob, oc, w_all, layer):
    m, d = x2d.shape
    d_mix = w_all.shape[1]
    assert oa.shape[1] + ob.shape[1] + oc.shape[1] == d_mix
    wide = 2 * d_mix * d * 2 <= VMEM_LIMIT // 2
    tn = d if wide else _pick(d, (512, 256, 128))
    tm = _pick(m, (512, 256, 128, 64, 32, 16, 8) if wide else (1024, 512, 256, 128, 64, 32, 16, 8))
    row = lambda a: pl.BlockSpec((tm, a.shape[1]), lambda i, j: (i, 0))
    return pl.pallas_call(
        _outproj_kernel,
        grid=(m // tm, d // tn),
        in_specs=[pl.BlockSpec((tm, tn), lambda i, j: (i, j)), row(oa), row(ob), row(oc),
                  pl.BlockSpec((None, d_mix, tn), lambda i, j: (layer, 0, j))],
        out_specs=pl.BlockSpec((tm, tn), lambda i, j: (i, j)),
        out_shape=jax.ShapeDtypeStruct((m, d), F32),
        compiler_params=_params("arbitrary", "arbitrary"),
        name="outproj",
    )(x2d, oa, ob, oc, w_all)


def _pool_tile(u_ref, g_ref, w_ref, scale_ref, o_ref, hist_out_ref, ext_ref, pos_first, tl, group):
    ext_ref[HIST_ROWS:HIST_ROWS + tl, :] = u_ref[...]
    pos = pos_first + lax.broadcasted_iota(jnp.int32, (tl, group), 0)
    for gi, w in enumerate(POOL_WINDOWS):
        cs = slice(gi * group, (gi + 1) * group)
        acc = ext_ref[HIST_ROWS:HIST_ROWS + tl, cs]
        for j in range(1, w):
            acc = acc + ext_ref[HIST_ROWS - j:HIST_ROWS - j + tl, cs]
        cnt = jnp.minimum(w, pos + 1).astype(F32)
        diff = acc / cnt - u_ref[:, cs]
        out = jnp.dot(diff.astype(BF16), w_ref[gi], preferred_element_type=F32) * scale_ref[:, cs]
        o_ref[:, cs] = (_silu(g_ref[:, cs]) * out).astype(BF16)
    hist_out_ref[...] = ext_ref[tl:tl + HIST_ROWS, :]


def _pool_kernel(u_ref, g_ref, hist_ref, w_ref, scale_ref, o_ref, hist_out_ref, ext_ref,
                 *, tl, pos0, group):
    t = pl.program_id(1)

    @pl.when(t == 0)
    def _():
        ext_ref[0:HIST_ROWS, :] = hist_ref[...]

    @pl.when(t > 0)
    def _():
        ext_ref[0:HIST_ROWS, :] = ext_ref[tl:tl + HIST_ROWS, :]

    _pool_tile(u_ref, g_ref, w_ref, scale_ref, o_ref, hist_out_ref, ext_ref, pos0 + t * tl, tl, group)


def _pool(z3, hist16, w_bf16, scale, pos0, d_pool):
    b, l, _ = z3.shape
    group = d_pool // len(POOL_WINDOWS)
    tl = _pick(l, (512, 256, 128, 64, 32, 16, 8))
    kern = functools.partial(_pool_kernel, tl=tl, pos0=pos0, group=group)
    return pl.pallas_call(
        kern,
        grid=(b, l // tl),
        in_specs=[pl.BlockSpec((None, tl, d_pool), lambda i, t: (i, t, 0)),
                  pl.BlockSpec((None, tl, d_pool), lambda i, t: (i, t, 1)),
                  pl.BlockSpec((None, HIST_ROWS, d_pool), lambda i, t: (i, 0, 0)),
                  pl.BlockSpec((len(POOL_WINDOWS), group, group), lambda i, t: (0, 0, 0)),
                  pl.BlockSpec((1, d_pool), lambda i, t: (0, 0))],
        out_specs=[pl.BlockSpec((None, tl, d_pool), lambda i, t: (i, t, 0)),
                   pl.BlockSpec((None, HIST_ROWS, d_pool), lambda i, t: (i, 0, 0))],
        out_shape=[jax.ShapeDtypeStruct((b, l, d_pool), BF16),
                   jax.ShapeDtypeStruct((b, HIST_ROWS, d_pool), F32)],
        scratch_shapes=[pltpu.VMEM((HIST_ROWS + tl, d_pool), F32)],
        compiler_params=_params("arbitrary", "arbitrary"),
        name="pool",
    )(z3, z3, hist16, w_bf16, scale)


def _cumsum_rows(x):
    n = x.shape[0]
    row = lax.broadcasted_iota(jnp.int32, x.shape, 0)
    s = 1
    while s < n:
        x = x + jnp.where(row >= s, pltpu.roll(x, s, axis=0), 0.0)
        s *= 2
    return x


def _hgrn_chunk(q_raw, z, v, lb_terms, st_ref, consts, chunk, sub, live=None):
    log_lb, log1m_lb, one_m_lb = lb_terms
    ones_kk, diag_mask, col_blk = consts
    lsig = jnp.minimum(z, 0.0) - jnp.log1p(jnp.exp(-jnp.abs(z)))
    b_ = log1m_lb + lsig
    logf = jnp.maximum(log_lb, b_) + jnp.log1p(jnp.exp(-jnp.abs(log_lb - b_)))
    k = one_m_lb * (1.0 / (1.0 + jnp.exp(z)))
    if live is not None:
        logf = jnp.where(live, logf, 0.0)
        k = jnp.where(live, k, 0.0)
    q = _silu(q_raw)
    cum = _cumsum_rows(logf)
    last = cum[chunk - 1:chunk, :]
    st = st_ref[...]
    o = lax.dot_general((q * jnp.exp(cum)).astype(BF16), st.astype(BF16),
                        (((1,), (1,)), ((), ())), preferred_element_type=F32)
    nsub = chunk // sub
    vb16 = v.astype(BF16)
    if nsub > 1:
        atts = [jnp.zeros((sub, chunk), F32)]
        for i in range(1, nsub):
            c_i = cum[i * sub - 1:i * sub, :]
            a = q[i * sub:(i + 1) * sub] * jnp.exp(cum[i * sub:(i + 1) * sub] - c_i)
            bmat = k * jnp.exp(jnp.minimum(c_i - cum, 0.0))
            att = lax.dot_general(a.astype(BF16), bmat.astype(BF16),
                                  (((1,), (1,)), ((), ())), preferred_element_type=F32)
            atts.append(jnp.where(col_blk < i, att, 0.0))
        att_all = jnp.concatenate(atts, axis=0)
        o = o + jnp.dot(att_all.astype(BF16), vb16, preferred_element_type=F32)
    diag = []
    for i in range(nsub):
        sl = slice(i * sub, (i + 1) * sub)
        cb, kb, qb, vb = cum[sl], k[sl], q[sl], v[sl]
        ws = []
        for t in range(sub):
            e = jnp.exp(jnp.minimum(cb[t:t + 1, :] - cb, 0.0))
            ws.append((qb[t:t + 1, :] * kb) * e)
        w = jnp.concatenate(ws, axis=0)
        att_rep = jnp.dot(w.astype(BF16), ones_kk, preferred_element_type=F32)
        p = jnp.where(diag_mask, att_rep * jnp.concatenate([vb] * sub, axis=0), 0.0)
        diag.append(jnp.sum(p.reshape(sub, sub, HEAD_DIM), axis=1))
    o = o + (jnp.concatenate(diag, axis=0) if nsub > 1 else diag[0])
    kd = (k * jnp.exp(last - cum)).astype(BF16)
    upd = lax.dot_general(vb16, kd, (((0,), (0,)), ((), ())), preferred_element_type=F32)
    st_ref[...] = st * jnp.exp(last) + upd
    return o


def _hgrn_consts(chunk, sub):
    ones_kk = jnp.ones((HEAD_DIM, HEAD_DIM), BF16)
    r = lax.broadcasted_iota(jnp.int32, (sub * sub, HEAD_DIM), 0)
    diag_mask = (r % sub) <= (r // sub)
    col_blk = lax.broadcasted_iota(jnp.int32, (sub, chunk), 1) // sub
    return ones_kk, diag_mask, col_blk


def _lb_terms(lb):
    return jnp.log(lb), jnp.log1p(-lb), 1.0 - lb


def _hgrn_halves(chunk):
    return [chunk >> (i + 1) for i in range(chunk.bit_length() - 1)]


def _hgrn_level_consts(chunk):
    halves = _hgrn_halves(chunk)
    nl = len(halves)
    m = np.zeros((nl + 1, chunk, chunk), np.float32)
    mask = np.zeros((nl, chunk, chunk), np.float32)
    for li, hz in enumerate(halves):
        for t in range(chunk):
            base = (t // (2 * hz)) * 2 * hz
            mid = base + hz
            if t >= mid:
                m[li, t, mid:t + 1] = 1.0
                mask[li, t, base:mid] = 1.0
            else:
                m[li, t, t + 1:mid] = 1.0
    m[nl] = np.tril(np.ones((chunk, chunk), np.float32))
    m3 = np.zeros(((nl + 1) * chunk, MXU_DEPTH), np.float32)
    m3[:, :3 * chunk] = np.tile(m.reshape((nl + 1) * chunk, chunk), (1, 3))
    return jnp.asarray(m3, BF16), jnp.asarray(mask, F32)


def _hgrn_prompt_consts(lb_ref, ng_ref, chunk):
    row = lax.broadcasted_iota(jnp.int32, (chunk, HEAD_DIM), 0)
    uppers = [(row % (2 * hz)) >= hz for hz in _hgrn_halves(chunk)]
    ones_kk = jnp.ones((HEAD_DIM, HEAD_DIM), BF16)
    kpad = jnp.zeros((MXU_DEPTH - 3 * chunk, HEAD_DIM), BF16)
    return _lb_terms(lb_ref[...]), ng_ref[...], uppers, ones_kk, kpad


def _run_phases(*phase_iters):
    live = list(phase_iters)
    while live:
        for it in list(live):
            if next(it, StopIteration) is StopIteration:
                live.remove(it)


def _hgrn_group(q_ref, f_ref, i_ref, g_ref, m3_ref, mask_ref, o_ref, st_ref, consts, rows, chunk):
    (log_lb, log1m_lb, one_m_lb), ng, uppers, ones_kk, kpad = consts
    nl = len(uppers)
    group = len(rows)
    qs, ks, vs, e2s = [], [], [], []
    for gi in range(group):
        z = f_ref[rows[gi], :]
        lsig = jnp.minimum(z, 0.0) - jnp.log1p(jnp.exp(-jnp.abs(z)))
        b_ = log1m_lb + lsig
        logf = jnp.maximum(log_lb, b_) + jnp.log1p(jnp.exp(-jnp.abs(log_lb - b_)))
        ks.append(one_m_lb * (1.0 / (1.0 + jnp.exp(z))))
        qs.append(_silu(q_ref[rows[gi], :]))
        vs.append(i_ref[rows[gi], :])
        g2 = logf * LOG2E
        hi = g2.astype(BF16)
        r1 = g2 - hi.astype(F32)
        mid = r1.astype(BF16)
        lo = (r1 - mid.astype(F32)).astype(BF16)
        e2s.append(jnp.dot(m3_ref[...], jnp.concatenate([hi, mid, lo, kpad], axis=0),
                           preferred_element_type=F32))
    yield
    atts = []
    for gi in range(group):
        att = None
        for li in range(nl):
            p = jnp.exp2(e2s[gi][li * chunk:(li + 1) * chunk])
            ab = (jnp.where(uppers[li], qs[gi], ks[gi]) * p).astype(BF16)
            a = lax.dot_general(ab, ab, _NT, preferred_element_type=F32) * mask_ref[li]
            att = a if att is None else att + a
        atts.append(att)
    yield
    outs, kds, qes, dec = [], [], [], []
    for gi in range(group):
        q, k, v = qs[gi], ks[gi], vs[gi]
        cum2 = e2s[gi][nl * chunk:(nl + 1) * chunk]
        last2 = cum2[chunk - 1:chunk, :]
        o = jnp.dot(atts[gi].astype(BF16), v.astype(BF16), preferred_element_type=F32)
        o = o + jnp.dot((q * k).astype(BF16), ones_kk, preferred_element_type=F32) * v
        outs.append(o)
        qes.append((q * jnp.exp2(cum2)).astype(BF16))
        kds.append((k * jnp.exp2(last2 - cum2)).astype(BF16))
        dec.append(jnp.exp2(last2))
    upds = [lax.dot_general(vs[gi].astype(BF16), kds[gi], (((0,), (0,)), ((), ())),
                            preferred_element_type=F32) for gi in range(group)]
    yield
    st = st_ref[...]
    for gi in range(group):
        o = outs[gi] + lax.dot_general(qes[gi], st.astype(BF16), _NT, preferred_element_type=F32)
        o_ref[rows[gi], :] = (_silu(g_ref[rows[gi], :]) * _rms_rows(o, ng)).astype(BF16)
        st = st * dec[gi] + upds[gi]
    st_ref[...] = st


def _hgrn_prompt_kernel(q_ref, f_ref, i_ref, g_ref, lb_ref, ng_ref, s0_ref, m3_ref, mask_ref,
                        o_ref, s_out_ref, st_ref, *, l, chunk, group):
    st_ref[...] = s0_ref[...].T
    consts = _hgrn_prompt_consts(lb_ref, ng_ref, chunk)

    def body(c, carry):
        rows = [pl.ds(pl.multiple_of(c * (chunk * group) + gi * chunk, chunk), chunk) for gi in range(group)]
        _run_phases(_hgrn_group(q_ref, f_ref, i_ref, g_ref, m3_ref, mask_ref, o_ref, st_ref, consts, rows, chunk))
        return carry

    lax.fori_loop(0, l // (chunk * group), body, 0)
    s_out_ref[...] = st_ref[...].T


def _hgrn_prompt(z3, lb, ng, s0, col0, n_heads):
    b, l, _ = z3.shape
    chunk = HGRN_CHUNK
    group = _pick(l // chunk, (16, 8, 4, 2, 1))
    m3, mask = _hgrn_level_consts(chunk)
    kern = functools.partial(_hgrn_prompt_kernel, l=l, chunk=chunk, group=group)
    zspec = lambda k: pl.BlockSpec((None, l, HEAD_DIM), lambda i, h: (i, 0, col0 + k * n_heads + h))
    hspec = pl.BlockSpec((1, HEAD_DIM), lambda i, h: (0, h))
    sspec = pl.BlockSpec((None, None, HEAD_DIM, HEAD_DIM), lambda i, h: (i, h, 0, 0))
    return pl.pallas_call(
        kern,
        grid=(b, n_heads),
        in_specs=[zspec(0), zspec(1), zspec(2), zspec(3), hspec, hspec, sspec,
                  pl.BlockSpec(m3.shape, lambda i, h: (0, 0)),
                  pl.BlockSpec(mask.shape, lambda i, h: (0, 0, 0))],
        out_specs=[pl.BlockSpec((None, l, HEAD_DIM), lambda i, h: (i, 0, h)), sspec],
        out_shape=[jax.ShapeDtypeStruct((b, l, n_heads * HEAD_DIM), BF16),
                   jax.ShapeDtypeStruct((b, n_heads, HEAD_DIM, HEAD_DIM), F32)],
        scratch_shapes=[pltpu.VMEM((HEAD_DIM, HEAD_DIM), F32)],
        compiler_params=_params("arbitrary", "arbitrary"),
        name="hgrn_prompt",
    )(z3, z3, z3, z3, lb, ng, s0, m3, mask)


def _sample_mixers_kernel(z_ref, hist_ref, pw_ref, pscale_ref, lb_ref, ng_ref, s0_ref,
                          oa_ref, hist_out_ref, ob_ref, s_out_ref, ext_ref, st_ref,
                          *, l, chunk, col0, n_heads, pos0, d_pool, seqs):
    consts = _hgrn_consts(chunk, chunk)
    live = lax.broadcasted_iota(jnp.int32, (chunk, HEAD_DIM), 0) < l
    pad = jnp.zeros((chunk - l, HEAD_DIM), F32)
    col = lambda k, h: slice((col0 + k * n_heads + h) * HEAD_DIM, (col0 + k * n_heads + h + 1) * HEAD_DIM)
    head = lambda h: slice(h * HEAD_DIM, (h + 1) * HEAD_DIM)
    for s in range(seqs):
        ext_ref[0:HIST_ROWS, :] = hist_ref[s]
        _pool_tile(z_ref.at[s, :, 0:d_pool], z_ref.at[s, :, d_pool:2 * d_pool], pw_ref, pscale_ref,
                   oa_ref.at[s], hist_out_ref.at[s], ext_ref, pos0, l, d_pool // len(POOL_WINDOWS))
        padded = lambda k, h: jnp.concatenate([z_ref[s, :, col(k, h)], pad], axis=0)
        for h in range(n_heads):
            st_ref[...] = s0_ref[s, h].T
            o = _hgrn_chunk(padded(0, h), padded(1, h), padded(2, h), _lb_terms(lb_ref[:, head(h)]), st_ref,
                            consts, chunk, chunk, live=live)
            o = _silu(z_ref[s, :, col(3, h)]) * _rms_rows(o[0:l], ng_ref[:, head(h)])
            ob_ref[s, :, head(h)] = o.astype(BF16)
            s_out_ref[s, h] = st_ref[...].T


def _sample_mixers(z3, hist16, w_pool, pool_scale, pos0, d_pool, lb, ng, s0_all, layer, col0, n_heads):
    b, l, ncols = z3.shape
    seqs = _pick(b, (4, 2, 1))
    kern = functools.partial(_sample_mixers_kernel, l=l, chunk=16, col0=col0, n_heads=n_heads, pos0=pos0,
                             d_pool=d_pool, seqs=seqs)
    d = n_heads * HEAD_DIM
    rows = lambda w: pl.BlockSpec((seqs, l, w), lambda i: (i, 0, 0))
    hist_spec = pl.BlockSpec((seqs, HIST_ROWS, d_pool), lambda i: (i, 0, 0))
    sspec = pl.BlockSpec((seqs, n_heads, HEAD_DIM, HEAD_DIM), lambda i: (i, 0, 0, 0))
    return pl.pallas_call(
        kern,
        grid=(b // seqs,),
        in_specs=[rows(ncols), hist_spec,
                  pl.BlockSpec(w_pool.shape, lambda i: (0, 0, 0)),
                  pl.BlockSpec((1, d_pool), lambda i: (0, 0)),
                  pl.BlockSpec((1, d), lambda i: (0, 0)),
                  pl.BlockSpec((1, d), lambda i: (0, 0)),
                  pl.BlockSpec((None, seqs, n_heads, HEAD_DIM, HEAD_DIM), lambda i: (layer, i, 0, 0, 0))],
        out_specs=[rows(d_pool), hist_spec, rows(d), sspec],
        out_shape=[jax.ShapeDtypeStruct((b, l, d_pool), BF16),
                   jax.ShapeDtypeStruct((b, HIST_ROWS, d_pool), F32),
                   jax.ShapeDtypeStruct((b, l, d), BF16),
                   jax.ShapeDtypeStruct((b, n_heads, HEAD_DIM, HEAD_DIM), F32)],
        scratch_shapes=[pltpu.VMEM((HIST_ROWS + l, d_pool), F32), pltpu.VMEM((HEAD_DIM, HEAD_DIM), F32)],
        compiler_params=_params("arbitrary"),
        name="sample_mixers",
    )(z3, hist16, w_pool, pool_scale, lb, ng, s0_all)


def _topk_select(sc, n_cand):
    lane = lax.broadcasted_iota(jnp.int32, sc.shape, 1)
    rank = jnp.zeros(sc.shape, F32)
    for m in range(n_cand):
        sc_m = sc[:, m:m + 1]
        ge = jnp.where(sc_m >= sc, 1.0, 0.0)
        gt = jnp.where(sc_m > sc, 1.0, 0.0)
        rank = rank + jnp.where(lane > m, ge, gt)
    return jnp.where(rank < MOBA_TOPK, 1.0, 0.0)


def _moba_prompt_kernel(slope_ref, q_ref, k_ref, v_ref, g_ref, qg_ref, kg_ref, *rest, l, nb, n_prev):
    if n_prev:
        kprev_ref, vprev_ref = rest[:2]
        rest = rest[2:]
    o_ref, ko_ref, vo_ref, qn_ref, qa_ref, ka_ref, vb_ref, km_ref, s_ref = rest
    blk, d = MOBA_BLOCK, HEAD_DIM
    nbp = -(-nb // 8) * 8
    slope2 = slope_ref[pl.program_id(1)] * LOG2E
    if n_prev:
        ko_ref[0:n_prev] = kprev_ref[...]
        vo_ref[0:n_prev] = vprev_ref[...]
    qn = _rms_rows(q_ref[...], qg_ref[...])
    qn_ref[...] = qn
    qa_ref[:, 0:d] = (qn * (d ** -0.5 * LOG2E)).astype(BF16)
    kn = _rms_rows(k_ref[...], kg_ref[...])
    ka_ref[:, 0:d] = kn.astype(BF16)
    key_blk = lax.broadcasted_iota(jnp.int32, (l, d), 0) // blk
    ka_ref[:, d:2 * d] = jnp.where(key_blk == lax.broadcasted_iota(jnp.int32, (l, d), 1), 1.0, 0.0).astype(BF16)
    ko_ref[n_prev] = kn.reshape(l // PAGE_SIZE, PAGE_SIZE, d)
    v = v_ref[...]
    vb_ref[...] = v.astype(BF16)
    vo_ref[n_prev] = v.reshape(l // PAGE_SIZE, PAGE_SIZE, d)
    km_ref[...] = jnp.zeros(km_ref.shape, F32)
    for n in range(nb):
        kblk = ko_ref[n_prev, n * PAGES_PER_BLOCK:(n + 1) * PAGES_PER_BLOCK].reshape(blk, d)
        km_ref[n:n + 1, :] = jnp.mean(kblk, axis=0, keepdims=True)
    causal = lax.broadcasted_iota(jnp.int32, (blk, blk), 1) <= lax.broadcasted_iota(jnp.int32, (blk, blk), 0)
    blk_row = lax.broadcasted_iota(jnp.int32, (nbp, blk), 0)

    def scores(i):
        rows = slice(i * blk, (i + 1) * blk)
        nk = (i + 1) * blk
        if i <= MOBA_TOPK:
            return lax.dot_general(qa_ref[rows, 0:d], ka_ref[0:nk, 0:d], _NT, preferred_element_type=F32)
        sct = lax.dot_general(km_ref[0:nbp, :], qn_ref[rows, :], _NT, preferred_element_type=F32,
                              precision=lax.Precision.HIGHEST)
        rank = jnp.zeros((nbp, blk), F32)
        for m in range(i):
            sm = sct[m:m + 1, :]
            ge = jnp.where(sm >= sct, 1.0, 0.0)
            gt = jnp.where(sm > sct, 1.0, 0.0)
            rank = rank + jnp.where(blk_row > m, ge, gt)
        drop = jnp.where((rank < MOBA_TOPK) | (blk_row >= i), 0.0, MASKED)
        drop = jnp.concatenate([drop, jnp.zeros((d - nbp, blk), F32)], axis=0).T
        qa_ref[rows, d:2 * d] = drop.astype(BF16)
        return lax.dot_general(qa_ref[rows, :], ka_ref[0:nk, :], _NT, preferred_element_type=F32)

    def stage(i):
        n_past = i * blk
        kpos = lax.broadcasted_iota(jnp.int32, (1, n_past + blk), 1) - n_past
        s = scores(i) + slope2 * kpos.astype(F32)
        if i:
            s_ref[i % 2, :, 0:n_past] = s[:, :n_past]
        s_ref[i % 2, :, n_past:n_past + blk] = jnp.where(causal, s[:, n_past:], MASKED)

    stage(0)
    for i in range(nb):
        nk = (i + 1) * blk
        if i + 1 < nb:
            stage(i + 1)
        for r0 in range(0, blk, SOFTMAX_ROWS):
            rows = slice(i * blk + r0, i * blk + r0 + SOFTMAX_ROWS)
            s = s_ref[i % 2, r0:r0 + SOFTMAX_ROWS, 0:nk]
            p = jnp.exp2(s - jnp.max(s, axis=-1, keepdims=True))
            den = jnp.sum(p, axis=-1, keepdims=True)
            o = jnp.dot(p.astype(BF16), vb_ref[0:nk, :], preferred_element_type=F32)
            o_ref[rows, :] = (_silu(g_ref[rows, :]) * (o / den)).astype(BF16)


def _moba_prompt(z3, qg, kg, slopes, col0, n_heads, prev=None):
    b, l, _ = z3.shape
    assert l % MOBA_BLOCK == 0
    nb = l // MOBA_BLOCK
    assert nb <= HEAD_DIM
    npg = l // PAGE_SIZE
    n_prev = 0 if prev is None else prev[0].shape[0]
    kern = functools.partial(_moba_prompt_kernel, l=l, nb=nb, n_prev=n_prev)
    zspec = lambda k: pl.BlockSpec((None, l, HEAD_DIM), lambda i, h: (i, 0, col0 + k * n_heads + h))
    gspec = pl.BlockSpec((1, HEAD_DIM), lambda i, h: (0, 0))
    pspec = lambda n: pl.BlockSpec((n, None, npg, None, PAGE_SIZE, HEAD_DIM), lambda i, h: (0, i, 0, h, 0, 0))
    page_shape = jax.ShapeDtypeStruct((n_prev + 1, b, npg, n_heads, PAGE_SIZE, HEAD_DIM), F32)
    prev_specs = [pspec(n_prev)] * 2 if n_prev else []
    return pl.pallas_call(
        kern,
        grid=(b, n_heads),
        in_specs=[pl.BlockSpec(memory_space=pltpu.SMEM),
                  zspec(0), zspec(1), zspec(2), zspec(3), gspec, gspec] + prev_specs,
        out_specs=[pl.BlockSpec((None, l, HEAD_DIM), lambda i, h: (i, 0, h)), pspec(n_prev + 1), pspec(n_prev + 1)],
        out_shape=[jax.ShapeDtypeStruct((b, l, n_heads * HEAD_DIM), BF16), page_shape, page_shape],
        scratch_shapes=[pltpu.VMEM((l, HEAD_DIM), F32),
                        pltpu.VMEM((l, 2 * HEAD_DIM), BF16),
                        pltpu.VMEM((l, 2 * HEAD_DIM), BF16),
                        pltpu.VMEM((l, HEAD_DIM), BF16),
                        pltpu.VMEM((HEAD_DIM, HEAD_DIM), F32),
                        pltpu.VMEM((2, MOBA_BLOCK, l), F32)],
        compiler_params=_params("arbitrary", "arbitrary"),
        name="moba_prompt",
    )(jnp.asarray(slopes, F32), z3, z3, z3, z3, qg, kg, *(prev or ()))


def _sample_parts(R, *, l, pp, n_heads, nb, col0, past, slopes):
    z_ref, qg_ref, kg_ref, k_refs, v_refs = R.z_ref, R.qg_ref, R.kg_ref, R.k_refs, R.v_refs
    o_ref, ko_ref, vo_ref = R.o_ref, R.ko_ref, R.vo_ref
    qn_ref, wq_ref, m_ref, l_ref, acc_ref, km_ref = R.qn_ref, R.wq_ref, R.m_ref, R.l_ref, R.acc_ref, R.km_ref
    blk = MOBA_BLOCK
    qscale = HEAD_DIM ** -0.5 * LOG2E
    col = lambda k, h: slice((col0 + k * n_heads + h) * HEAD_DIM, (col0 + k * n_heads + h + 1) * HEAD_DIM)
    zpad = jnp.zeros((HEAD_DIM - l, HEAD_DIM), F32)

    def init():
        for h in range(n_heads):
            qn = _rms_rows(z_ref[:, col(0, h)], qg_ref[...])
            qn_ref[h] = qn
            qt = jnp.concatenate([qn * qscale, zpad], axis=0).T
            if h:
                qt = pltpu.roll(qt, h * l, axis=1)
            wq_ref[h * HEAD_DIM:(h + 1) * HEAD_DIM, :] = qt.astype(BF16)
            ko_ref[h] = _rms_rows(z_ref[:, col(1, h)], kg_ref[...])
            vo_ref[h] = z_ref[:, col(2, h)]
        km_ref[...] = jnp.zeros(km_ref.shape, F32)
        m_ref[...] = jnp.zeros(m_ref.shape, F32)
        l_ref[...] = jnp.zeros(l_ref.shape, F32)

    def step(j):
        lane = lax.broadcasted_iota(jnp.int32, (1, HEAD_DIM), 1)
        slope2 = jnp.zeros((1, HEAD_DIM), F32)
        for h in range(n_heads):
            slope2 = jnp.where(lane // l == h, slopes[h] * LOG2E, slope2)
        lq = (lane % l).astype(F32)
        key_bias = slope2 * lax.broadcasted_iota(jnp.int32, (blk, HEAD_DIM), 0).astype(F32)
        bps = pp // PAGES_PER_BLOCK
        pages = lambda bi: range(bi * PAGES_PER_BLOCK, (bi + 1) * PAGES_PER_BLOCK)
        scores = []
        for bi in range(bps):
            kcat = jnp.concatenate(
                [jnp.concatenate([k_refs[p][h].astype(BF16) for h in range(n_heads)], axis=1) for p in pages(bi)],
                axis=0)
            scores.append(jnp.dot(kcat, wq_ref[...], preferred_element_type=F32))
        yield
        probs = []
        for bi in range(bps):
            n = j * bps + bi
            s = scores[bi] + key_bias + slope2 * ((n * blk - past).astype(F32) - lq)
            m = jnp.max(s, axis=0, keepdims=True)
            p_ = jnp.exp2(s - m)
            m_ref[pl.ds(n, 1), :] = m
            l_ref[pl.ds(n, 1), :] = jnp.sum(p_, axis=0, keepdims=True)
            probs.append(p_.T)
            for h in range(n_heads):
                kblk = jnp.concatenate([k_refs[p][h] for p in pages(bi)], axis=0)
                km_ref[h, pl.ds(n, 1), :] = jnp.mean(kblk, axis=0, keepdims=True)
        yield
        for bi in range(bps):
            n = j * bps + bi
            for h in range(n_heads):
                vblk = jnp.concatenate([v_refs[p][h].astype(BF16) for p in pages(bi)], axis=0)
                acc_ref[n, h * l:(h + 1) * l, :] = jnp.dot(probs[bi][h * l:(h + 1) * l, :].astype(BF16), vblk,
                                                           preferred_element_type=F32)

    def final():
        hl = n_heads * l
        ro = lax.broadcasted_iota(jnp.int32, (hl, HEAD_DIM), 0)
        co = lax.broadcasted_iota(jnp.int32, (hl, HEAD_DIM), 1)
        qi = ro % l
        slope_rows = jnp.zeros((hl, HEAD_DIM), F32)
        for h in range(n_heads):
            slope_rows = jnp.where(ro // l == h, slopes[h] * LOG2E, slope_rows)
        sc, s, vown = [], [], []
        for h in range(n_heads):
            qn = qn_ref[h]
            sc.append(lax.dot_general(qn, km_ref[h], _NT, preferred_element_type=F32,
                                      precision=lax.Precision.HIGHEST))
            kown = jnp.concatenate([ko_ref[h], zpad], axis=0).astype(BF16)
            vown.append(jnp.concatenate([vo_ref[h], zpad], axis=0).astype(BF16))
            s.append(lax.dot_general((qn * qscale).astype(BF16), kown, _NT, preferred_element_type=F32))
        keep = (_topk_select(jnp.concatenate(sc, axis=0), nb) > 0.5) & (co < nb)
        mh = m_ref[...].T[0:hl, :]
        lh = l_ref[...].T[0:hl, :]
        s = jnp.concatenate(s, axis=0)
        s = jnp.where(co <= qi, s - slope_rows * (qi - co).astype(F32), MASKED)
        mx = jnp.maximum(jnp.max(s, axis=-1, keepdims=True),
                         jnp.max(jnp.where(keep, mh, MASKED), axis=-1, keepdims=True))
        w = jnp.where(keep, jnp.exp2(mh - mx), 0.0)
        p_ = jnp.exp2(s - mx)
        den = jnp.sum(p_, axis=-1, keepdims=True) + jnp.sum(w * lh, axis=-1, keepdims=True)
        num = jnp.concatenate(
            [jnp.dot(p_[h * l:(h + 1) * l].astype(BF16), vown[h], preferred_element_type=F32)
             for h in range(n_heads)], axis=0)
        for n in range(nb):
            num = num + w[:, n:n + 1] * acc_ref[n]
        o = num / den
        for h in range(n_heads):
            hs = slice(h * HEAD_DIM, (h + 1) * HEAD_DIM)
            o_ref[:, hs] = (_silu(z_ref[:, col(3, h)]) * o[h * l:(h + 1) * l]).astype(BF16)

    return init, step, final


def _unpack_sample_refs(z_ref, qg_ref, kg_ref, pages, outs, scratch, pp):
    return types.SimpleNamespace(
        z_ref=z_ref, qg_ref=qg_ref, kg_ref=kg_ref, k_refs=pages[:pp], v_refs=pages[pp:],
        o_ref=outs[0], ko_ref=outs[1], vo_ref=outs[2],
        qn_ref=scratch[0], wq_ref=scratch[1], m_ref=scratch[2], l_ref=scratch[3], acc_ref=scratch[4],
        km_ref=scratch[5])


def _moba_sample_kernel(pt_ref, z_ref, qg_ref, kg_ref, *rest, pp, **static):
    del pt_ref
    refs = _unpack_sample_refs(z_ref, qg_ref, kg_ref, rest[:2 * pp], rest[2 * pp:2 * pp + 3], rest[2 * pp + 3:], pp)
    init, step, final = _sample_parts(refs, pp=pp, **static)
    j = pl.program_id(1)
    pl.when(j == 0)(init)
    _run_phases(step(j))
    pl.when(j == pl.num_programs(1) - 1)(final)


def _sample_hgrn_kernel(pt_ref, z_ref, qg_ref, kg_ref, *rest, pp, chunk, hsteps, n_hsteps, n_psteps, **static):
    del pt_ref
    hq_ref, hf_ref, hi_ref, hg_ref, lb_ref, ng_ref, s0_ref, m3_ref, mask_ref = rest[2 * pp:2 * pp + 9]
    pu_ref, pg_ref, pw_ref, pscale_ref = rest[2 * pp + 9:2 * pp + 13]
    outs = rest[2 * pp + 13:2 * pp + 20]
    scratch = rest[2 * pp + 20:]
    ho_ref, hs_ref, po_ref, phist_ref = outs[3:7]
    st_ref, ext_ref = scratch[6:8]
    refs = _unpack_sample_refs(z_ref, qg_ref, kg_ref, rest[:2 * pp], outs[:3], scratch[:6], pp)
    init, step, final = _sample_parts(refs, pp=pp, **static)
    j = pl.program_id(1)
    s = pl.program_id(0) * pl.num_programs(1) + j
    in_hgrn = s < n_hsteps
    in_pool = jnp.logical_and(s >= n_hsteps, s < n_hsteps + n_psteps)
    t = s % hsteps
    tp = (s - n_hsteps) % hsteps
    tl = ho_ref.shape[0]
    rows = [pl.ds(gi * chunk, chunk) for gi in range(tl // chunk)]
    pl.when(j == 0)(init)

    @pl.when(in_hgrn & (t == 0))
    def _():
        st_ref[...] = s0_ref[...].T

    @pl.when(in_pool & (tp == 0))
    def _():
        ext_ref[0:HIST_ROWS, :] = jnp.zeros((HIST_ROWS, ext_ref.shape[1]), F32)

    @pl.when(in_pool & (tp > 0))
    def _():
        ext_ref[0:HIST_ROWS, :] = ext_ref[tl:tl + HIST_ROWS, :]

    @pl.when(in_hgrn)
    def _():
        consts = _hgrn_prompt_consts(lb_ref, ng_ref, chunk)
        _run_phases(_hgrn_group(hq_ref, hf_ref, hi_ref, hg_ref, m3_ref, mask_ref, ho_ref, st_ref, consts, rows, chunk),
                    step(j))

    @pl.when(in_pool)
    def _():
        _run_phases(step(j))
        _pool_tile(pu_ref, pg_ref, pw_ref, pscale_ref, po_ref, phist_ref, ext_ref, tp * tl, tl,
                   pu_ref.shape[1] // len(POOL_WINDOWS))

    @pl.when(s >= n_hsteps + n_psteps)
    def _():
        _run_phases(step(j))

    @pl.when(in_hgrn & (t == hsteps - 1))
    def _():
        hs_ref[...] = st_ref[...].T

    pl.when(j == pl.num_programs(1) - 1)(final)


def _sample_page_group(n_pages):
    return _pick(n_pages, (16, 8, 4, 2))


def _can_ride(b, l, n_heads, sb, n_pages):
    if l % HGRN_RIDE_ROWS:
        return False
    return b * (n_heads + 1) * (l // HGRN_RIDE_ROWS) <= sb * (n_pages // _sample_page_group(n_pages))


def _moba_sample(z3, qg, kg, cache_k, cache_v, page_table, layer, slopes, col0, n_heads, ride=None):
    b, l, ncols = z3.shape
    n_pages = page_table.shape[1]
    assert n_pages % PAGES_PER_BLOCK == 0, "the own block must hold no cached page"
    nb = n_pages // PAGES_PER_BLOCK
    assert nb <= HEAD_DIM
    assert n_heads * l <= HEAD_DIM
    pp = _sample_page_group(n_pages)
    nj = n_pages // pp
    static = dict(l=l, pp=pp, n_heads=n_heads, nb=nb, col0=col0, past=n_pages * PAGE_SIZE, slopes=slopes)
    d = n_heads * HEAD_DIM
    gspec = pl.BlockSpec((1, HEAD_DIM), lambda i, j, pt: (0, 0))

    def page_spec(p):
        return pl.BlockSpec((None, None, n_heads, PAGE_SIZE, HEAD_DIM),
                            lambda i, j, pt: (layer, pt[i, j * pp + p], 0, 0, 0))

    new_spec = pl.BlockSpec((None, n_heads, l, HEAD_DIM), lambda i, j, pt: (i, 0, 0, 0))
    new_shape = jax.ShapeDtypeStruct((b, n_heads, l, HEAD_DIM), F32)
    stat = pltpu.VMEM((HEAD_DIM, HEAD_DIM), F32)
    in_specs = ([pl.BlockSpec((None, l, ncols), lambda i, j, pt: (i, 0, 0)), gspec, gspec]
                + [page_spec(p) for p in range(pp)] * 2)
    out_specs = [pl.BlockSpec((None, l, d), lambda i, j, pt: (i, 0, 0)), new_spec, new_spec]
    out_shape = [jax.ShapeDtypeStruct((b, l, d), BF16), new_shape, new_shape]
    scratch = [pltpu.VMEM((n_heads, l, HEAD_DIM), F32),
               pltpu.VMEM((n_heads * HEAD_DIM, HEAD_DIM), BF16),
               stat, stat,
               pltpu.VMEM((nb, n_heads * l, HEAD_DIM), F32),
               pltpu.VMEM((n_heads, HEAD_DIM, HEAD_DIM), F32)]
    args = [page_table, z3, qg, kg, *([cache_k] * pp), *([cache_v] * pp)]
    if ride is None:
        kern = functools.partial(_moba_sample_kernel, **static)
        name = "moba_sample"
    else:
        zp, lb, ng, s0, hcol0, hh, w_pool, pool_scale, d_pool = ride
        pb, pl_rows, _ = zp.shape
        assert _can_ride(pb, pl_rows, hh, b, n_pages)
        tl = HGRN_RIDE_ROWS
        hsteps = pl_rows // tl
        n_hsteps = pb * hh * hsteps
        n_psteps = pb * hsteps
        m3, mask = _hgrn_level_consts(HGRN_CHUNK)

        def pool_tile(i, j):
            q = jnp.clip(i * nj + j - n_hsteps, 0, n_psteps - 1)
            return q // hsteps, q % hsteps

        def where(i, j):
            s = jnp.minimum(i * nj + j, n_hsteps - 1)
            return (s // hsteps) // hh, (s // hsteps) % hh, s % hsteps

        def zspec(k):
            def index(i, j, pt):
                bp, h, t = where(i, j)
                return bp, t, hcol0 + k * hh + h
            return pl.BlockSpec((None, tl, HEAD_DIM), index)

        hspec = pl.BlockSpec((1, HEAD_DIM), lambda i, j, pt: (0, where(i, j)[1]))
        sspec = pl.BlockSpec((None, None, HEAD_DIM, HEAD_DIM),
                             lambda i, j, pt: (where(i, j)[0], where(i, j)[1], 0, 0))
        ospec = pl.BlockSpec((None, tl, HEAD_DIM),
                             lambda i, j, pt: (where(i, j)[0], where(i, j)[2], where(i, j)[1]))
        pool_in = lambda c: pl.BlockSpec((None, tl, d_pool), lambda i, j, pt: (*pool_tile(i, j), c))
        hist_spec = pl.BlockSpec((None, HIST_ROWS, d_pool), lambda i, j, pt: (pool_tile(i, j)[0], 0, 0))
        in_specs += [zspec(0), zspec(1), zspec(2), zspec(3), hspec, hspec, sspec,
                     pl.BlockSpec(m3.shape, lambda i, j, pt: (0, 0)),
                     pl.BlockSpec(mask.shape, lambda i, j, pt: (0, 0, 0)),
                     pool_in(0), pool_in(1),
                     pl.BlockSpec(w_pool.shape, lambda i, j, pt: (0, 0, 0)),
                     pl.BlockSpec((1, d_pool), lambda i, j, pt: (0, 0))]
        out_specs += [ospec, sspec, pool_in(0), hist_spec]
        out_shape += [jax.ShapeDtypeStruct((pb, pl_rows, hh * HEAD_DIM), BF16),
                      jax.ShapeDtypeStruct((pb, hh, HEAD_DIM, HEAD_DIM), F32),
                      jax.ShapeDtypeStruct((pb, pl_rows, d_pool), BF16),
                      jax.ShapeDtypeStruct((pb, HIST_ROWS, d_pool), F32)]
        scratch += [pltpu.VMEM((HEAD_DIM, HEAD_DIM), F32), pltpu.VMEM((HIST_ROWS + tl, d_pool), F32)]
        args += [zp, zp, zp, zp, lb, ng, s0, m3, mask, zp, zp, w_pool, pool_scale]
        kern = functools.partial(_sample_hgrn_kernel, chunk=HGRN_CHUNK, hsteps=hsteps, n_hsteps=n_hsteps,
                                 n_psteps=n_psteps, **static)
        name = "moba_sample_prompt_riders"
    return pl.pallas_call(
        kern,
        grid_spec=pltpu.PrefetchScalarGridSpec(
            num_scalar_prefetch=1, grid=(b, nj), in_specs=in_specs, out_specs=out_specs,
            scratch_shapes=scratch),
        out_shape=out_shape,
        compiler_params=_params("arbitrary", "arbitrary"),
        name=name,
    )(*args)


def _alibi_slopes(n):
    def geo(m):
        start = 2.0 ** (-8.0 / m)
        return [start ** (i + 1) for i in range(m)]
    if (n & (n - 1)) == 0:
        return geo(n)
    c = 2 ** int(math.floor(math.log2(n)))
    return geo(c) + geo(2 * c)[0::2][: n - c]


def kernel(x_prompt, x_sample, cache_k, cache_v, state_hgrn, state_pool, page_table, norm_gain, w_in,
           pool_w, pool_scale, hgrn_lb, hgrn_norm_gain, q_norm_gain, k_norm_gain, w_out):
    depth, d, _ = w_in.shape
    d_pool = d // 4
    d_hgrn = (d - d_pool) // 2
    d_att = d - d_pool - d_hgrn
    h_hgrn, h_att = d_hgrn // HEAD_DIM, d_att // HEAD_DIM
    col_hgrn = 2 * d_pool // HEAD_DIM
    col_att = col_hgrn + 4 * h_hgrn
    b, t, _ = x_prompt.shape
    sb, sl, _ = x_sample.shape
    past = page_table.shape[1] * PAGE_SIZE
    slopes = [float(np.float32(s)) for s in _alibi_slopes(h_att)]
    lb_cum = jnp.cumsum(jax.nn.softmax(hgrn_lb.astype(F32), axis=0), axis=0)
    lb_all = lb_cum - lb_cum[:1]
    zero_hist = jnp.zeros((b, HIST_ROWS, d_pool), F32)
    zero_state = jnp.zeros((b, h_hgrn, HEAD_DIM, HEAD_DIM), F32)

    y_p = x_prompt.reshape(b * t, d)
    y_s = x_sample.reshape(sb * sl, d)
    outs = [[] for _ in range(6)]
    kv_pages = None
    cast_in_call = sb * sl <= 512
    w_i = None if cast_in_call else w_in.astype(BF16)
    w_o = w_out.astype(BF16)
    for layer in range(depth):
        w_p = pool_w[layer].astype(BF16)
        gain = norm_gain[layer][None]
        scale = pool_scale[layer][None]
        lb, ng = lb_all[layer][None], hgrn_norm_gain[layer][None]
        qg, kg = q_norm_gain[layer][None], k_norm_gain[layer][None]

        if cast_in_call:
            zs, w_l = _inproj_cast(y_s, gain, w_in, layer)
            z = _inproj(y_p, gain, w_l, 0).reshape(b, t, -1)
        else:
            zs = _inproj(y_s, gain, w_i, layer)
            z = _inproj(y_p, gain, w_i, layer).reshape(b, t, -1)
        zs = zs.reshape(sb, sl, -1)
        sample_args = (zs, qg, kg, cache_k, cache_v, page_table, layer, slopes, col_att, h_att)
        if _can_ride(b, t, h_hgrn, sb, page_table.shape[1]):
            oc_s, k_s, v_s, ob, s_p, oa, hist_p = _moba_sample(
                *sample_args, ride=(z, lb, ng, zero_state, col_hgrn, h_hgrn, w_p, scale, d_pool))
        else:
            oc_s, k_s, v_s = _moba_sample(*sample_args)
            ob, s_p = _hgrn_prompt(z, lb, ng, zero_state, col_hgrn, h_hgrn)
            oa, hist_p = _pool(z, zero_hist, w_p, scale, 0, d_pool)
        oc, *kv_pages = _moba_prompt(z, qg, kg, slopes, col_att, h_att, prev=kv_pages)
        y_p = _outproj(y_p, oa.reshape(b * t, -1), ob.reshape(b * t, -1), oc.reshape(b * t, -1), w_o, layer)

        hist16 = jnp.pad(state_pool[layer], ((0, 0), (HIST_ROWS - POOL_HIST, 0), (0, 0)))
        oa, hist_s, ob, s_s = _sample_mixers(zs, hist16, w_p, scale, past, d_pool, lb, ng, state_hgrn, layer,
                                             col_hgrn, h_hgrn)
        y_s = _outproj(y_s, oa.reshape(sb * sl, -1), ob.reshape(sb * sl, -1), oc_s.reshape(sb * sl, -1), w_o, layer)

        for lst, a in zip(outs, (s_p, hist_p[:, 1:], k_s, v_s, s_s, hist_s[:, 1:])):
            lst.append(a)
    s_p, hist_p, k_s, v_s, s_s, hist_s = (jnp.stack(o) for o in outs)
    return (y_p.reshape(b, t, d), y_s.reshape(sb, sl, d), kv_pages[0], kv_pages[1], s_p, hist_p,
            k_s, v_s, s_s, hist_s)
```
